```python
import functools
import jax, jax.numpy as jnp
from jax import lax
import numpy as np

D_MODEL = 1024
BATCH = 8
SEQ = 4096
DEPTH = 1
DEC_BATCH = 128
DEC_SEQ = 4
PAST_LEN = 8192
PAGE_SIZE = 128

CHUNK = 128
A_DIM = D_MODEL
A_GROUPS = 8
A_GROUP_DIM = A_DIM // A_GROUPS
N_HEADS = 8
N_KV_HEADS = 4
HEAD_DIM = D_MODEL // N_HEADS
Q_DIM = N_HEADS * HEAD_DIM
KV_DIM = N_KV_HEADS * HEAD_DIM
N_IDX_HEADS = 8
IDX_DIM = 64
TOPK_MAX = 256
Q_BLOCK = 128
D_FF = -(-8 * D_MODEL // (3 * 256)) * 256
ALPHA = (2 * DEPTH) ** 0.25
BETA = (8 * DEPTH) ** -0.25
LN_EPS = 1e-5
SPLIT_SIZES = (A_DIM, A_DIM, Q_DIM, KV_DIM, KV_DIM, N_IDX_HEADS * IDX_DIM, IDX_DIM, N_IDX_HEADS, D_MODEL, D_MODEL)
IN_COLS = 2 * A_DIM + Q_DIM + 2 * KV_DIM + N_IDX_HEADS * IDX_DIM + IDX_DIM + N_IDX_HEADS + 2 * D_MODEL
V_COL_START = 2 * A_DIM + Q_DIM + KV_DIM

kernel_name = "gated_gmlp_dsa_hybrid_step"


def layer_norm(x, g, b):
    xf = x.astype(jnp.float32)
    mu = xf.mean(-1, keepdims=True)
    var = jnp.square(xf - mu).mean(-1, keepdims=True)
    return ((xf - mu) * lax.rsqrt(var + LN_EPS) * g + b).astype(x.dtype)


def split_projection(h):
    pieces, start = [], 0
    for n in SPLIT_SIZES:
        pieces.append(h[..., start:start + n])
        start += n
    return pieces


def chunk_spatial_mix(v, w_s, b_s):
    L = v.shape[2]
    w = jnp.tril(w_s[:, :L, :L])
    s = jnp.einsum('gij,bcjgd->bcigd', w, v)
    return s + b_s[:, :L].T[None, None, :, :, None]


def mixer_a(u, v, ln_v_g, ln_v_b, w_s, b_s):
    B, T, _ = u.shape
    L = min(T, CHUNK)
    v_n = layer_norm(v, ln_v_g, ln_v_b)
    s = chunk_spatial_mix(v_n.reshape(B, T // L, L, A_GROUPS, A_GROUP_DIM), w_s, b_s)
    return u * s.reshape(B, T, A_DIM), v_n


def indexer_scores(q_idx, w_idx, k_idx):
    dots = jnp.einsum('bqhd,bsd->bqsh', q_idx, k_idx, preferred_element_type=jnp.float32)
    return jnp.einsum('bqsh,bqh->bqs', jax.nn.relu(dots * IDX_DIM ** -0.5), w_idx.astype(jnp.float32))


def select_keys(scores, q_pos, n_keys):
    key_pos = jnp.arange(n_keys)
    masked = jnp.where(key_pos[None, None, :] <= q_pos[None, :, None], scores, -jnp.inf)
    _, idx = lax.top_k(masked, min(TOPK_MAX, n_keys // 4))
    return idx, idx <= q_pos[None, :, None]


def sparse_attend(q, k_sel, v_sel, valid):
    B, Q = q.shape[:2]
    qg = q.reshape(B, Q, N_KV_HEADS, N_HEADS // N_KV_HEADS, HEAD_DIM)
    logits = jnp.einsum('bqhgd,bqnhd->bqhgn', qg, k_sel, preferred_element_type=jnp.float32) * HEAD_DIM ** -0.5
    logits = jnp.where(valid[:, :, None, None, :], logits, -jnp.inf)
    p = jax.nn.softmax(logits, axis=-1).astype(v_sel.dtype)
    o = jnp.einsum('bqhgn,bqnhd->bqhgd', p, v_sel)
    return o.reshape(B, Q, Q_DIM)


gather_rows = jax.vmap(lambda rows, idx: rows[idx])


def mixer_b_prompt(q, k, v, q_idx, w_idx, k_idx):
    B, T = q.shape[:2]
    nb = T // Q_BLOCK

    def to_blocks(a):
        return a.reshape(B, nb, Q_BLOCK, *a.shape[2:]).swapaxes(0, 1)

    def block(args):
        qb, qib, wb, t0 = args
        q_pos = t0 + jnp.arange(Q_BLOCK)
        idx, valid = select_keys(indexer_scores(qib, wb, k_idx), q_pos, T)
        return sparse_attend(qb, gather_rows(k, idx), gather_rows(v, idx), valid)

    out = lax.map(block, (to_blocks(q), to_blocks(q_idx), to_blocks(w_idx), jnp.arange(nb) * Q_BLOCK))
    return out.swapaxes(0, 1).reshape(B, T, Q_DIM)


def mixer_b_sample(q, k, v, q_idx, w_idx, k_idx, cache_k, cache_v, cache_kidx, page_table, layer):
    DB, T = q.shape[:2]
    past = page_table.shape[1] * PAGE_SIZE
    past_kidx = cache_kidx[layer, page_table].reshape(DB, past, IDX_DIM)
    all_kidx = jnp.concatenate([past_kidx, k_idx], axis=1)
    q_pos = past + jnp.arange(T)
    idx, valid = select_keys(indexer_scores(q_idx, w_idx, all_kidx), q_pos, past + T)
    is_new = (idx >= past)[..., None, None]
    pidx = jnp.minimum(idx, past - 1)
    phys = page_table[jnp.arange(DB)[:, None, None], pidx // PAGE_SIZE]
    off = pidx % PAGE_SIZE
    nidx = jnp.clip(idx - past, 0, T - 1)
    k_sel = jnp.where(is_new, gather_rows(k, nidx), cache_k[layer, phys, off])
    v_sel = jnp.where(is_new, gather_rows(v, nidx), cache_v[layer, phys, off])
    return sparse_attend(q, k_sel, v_sel, valid)


def trunk_layer(x, attend_fn, w_in, ln_v_g, ln_v_b, w_s, b_s, w_pa, w_pb, w_o,
                ln1_g, ln1_b, w_gate, w_up, w_down, ln2_g, ln2_b):
    B, T, _ = x.shape
    z_u, z_v, q, k, v, q_idx, k_idx, w_idx, g_a, g_b = split_projection(x @ w_in)
    a_out, v_norm = mixer_a(jax.nn.gelu(z_u), jax.nn.gelu(z_v), ln_v_g, ln_v_b, w_s, b_s)
    q = q.reshape(B, T, N_HEADS, HEAD_DIM)
    k = k.reshape(B, T, N_KV_HEADS, HEAD_DIM)
    v = v.reshape(B, T, N_KV_HEADS, HEAD_DIM)
    q_idx = q_idx.reshape(B, T, N_IDX_HEADS, IDX_DIM)
    w_idx = w_idx * N_IDX_HEADS ** -0.5
    b_out = attend_fn(q, k, v, q_idx, w_idx, k_idx)
    mix = (jax.nn.sigmoid(g_a) * (a_out @ w_pa) + jax.nn.sigmoid(g_b) * (b_out @ w_pb)) @ w_o
    x1 = layer_norm(ALPHA * x + mix, ln1_g, ln1_b)
    f = (jax.nn.silu(x1 @ w_gate) * (x1 @ w_up)) @ w_down
    y = layer_norm(ALPHA * x1 + f, ln2_g, ln2_b)
    return y, k, v, k_idx, v_norm


def setup_inputs(seed: int = 0) -> dict:
    key = jax.random.key(seed)
    ks = jax.random.split(key, 24)
    n_pages = PAST_LEN // PAGE_SIZE
    n_pool = (DEC_BATCH * n_pages * 5) // 4

    def nrm(k, shape, scale):
        return jax.random.normal(k, shape, jnp.float32) * scale

    w_in = nrm(ks[0], (DEPTH, D_MODEL, IN_COLS), D_MODEL ** -0.5)
    w_in = w_in.at[:, :, V_COL_START:V_COL_START + KV_DIM].multiply(BETA)
    page_table = jax.random.permutation(ks[1], n_pool)[:DEC_BATCH * n_pages]
    page_table = page_table.reshape(DEC_BATCH, n_pages).astype(jnp.int32)
    return {
        "x_prompt": nrm(ks[2], (BATCH, SEQ, D_MODEL), 1.0),
        "x_sample": nrm(ks[3], (DEC_BATCH, DEC_SEQ, D_MODEL), 1.0),
        "cache_k": nrm(ks[4], (DEPTH, n_pool, PAGE_SIZE, N_KV_HEADS, HEAD_DIM), 1.0),
        "cache_v": nrm(ks[5], (DEPTH, n_pool, PAGE_SIZE, N_KV_HEADS, HEAD_DIM), BETA),
        "cache_kidx": nrm(ks[6], (DEPTH, n_pool, PAGE_SIZE, IDX_DIM), 1.0),
        "page_table": page_table,
        "w_in": w_in,
        "ln_v_g": 1.0 + nrm(ks[7], (DEPTH, A_DIM), 0.1),
        "ln_v_b": nrm(ks[8], (DEPTH, A_DIM), 0.1),
        "w_s": nrm(ks[9], (DEPTH, A_GROUPS, CHUNK, CHUNK), CHUNK ** -0.5),
        "b_s": 1.0 + nrm(ks[10], (DEPTH, A_GROUPS, CHUNK), 0.1),
        "w_pa": nrm(ks[11], (DEPTH, A_DIM, D_MODEL), A_DIM ** -0.5 * BETA),
        "w_pb": nrm(ks[12], (DEPTH, Q_DIM, D_MODEL), Q_DIM ** -0.5 * BETA),
        "w_o": nrm(ks[13], (DEPTH, D_MODEL, D_MODEL), D_MODEL ** -0.5 * BETA),
        "ln1_g": 1.0 + nrm(ks[14], (DEPTH, D_MODEL), 0.1),
        "ln1_b": nrm(ks[15], (DEPTH, D_MODEL), 0.1),
        "w_gate": nrm(ks[16], (DEPTH, D_MODEL, D_FF), D_MODEL ** -0.5 * BETA),
        "w_up": nrm(ks[17], (DEPTH, D_MODEL, D_FF), D_MODEL ** -0.5 * BETA),
        "w_down": nrm(ks[18], (DEPTH, D_FF, D_MODEL), D_FF ** -0.5 * BETA),
        "ln2_g": 1.0 + nrm(ks[19], (DEPTH, D_MODEL), 0.1),
        "ln2_b": nrm(ks[20], (DEPTH, D_MODEL), 0.1),
    }


def reference(x_prompt, x_sample, cache_k, cache_v, cache_kidx, page_table, w_in, ln_v_g, ln_v_b,
              w_s, b_s, w_pa, w_pb, w_o, ln1_g, ln1_b, w_gate, w_up, w_down, ln2_g, ln2_b):
    y_prompt, y_sample = x_prompt, x_sample
    kp, vp, ip, ksl, vsl, isl, csl = [], [], [], [], [], [], []
    for layer in range(DEPTH):
        wl = [w[layer] for w in (w_in, ln_v_g, ln_v_b, w_s, b_s, w_pa, w_pb, w_o,
                                 ln1_g, ln1_b, w_gate, w_up, w_down, ln2_g, ln2_b)]
        y_prompt, k_p, v_p, i_p, _ = trunk_layer(y_prompt, mixer_b_prompt, *wl)
        attend_s = functools.partial(mixer_b_sample, cache_k=cache_k, cache_v=cache_v,
                                     cache_kidx=cache_kidx, page_table=page_table, layer=layer)
        y_sample, k_s, v_s, i_s, c_s = trunk_layer(y_sample, attend_s, *wl)
        kp.append(k_p); vp.append(v_p); ip.append(i_p)
        ksl.append(k_s); vsl.append(v_s); isl.append(i_s); csl.append(c_s)
    k_prompt, v_prompt, kidx_prompt = jnp.stack(kp), jnp.stack(vp), jnp.stack(ip)
    k_sample, v_sample, kidx_sample = jnp.stack(ksl), jnp.stack(vsl), jnp.stack(isl)
    chunk_v_sample = jnp.stack(csl)
    return (y_prompt, y_sample, k_prompt, v_prompt, kidx_prompt, k_sample, v_sample, kidx_sample, chunk_v_sample)
```

```python
import functools
import math

import jax
import jax.numpy as jnp
from jax import lax
from jax.experimental import pallas as pl
from jax.experimental.pallas import tpu as pltpu

D_MODEL = 1024
CHUNK = 128
A_GROUPS = 8
A_GROUP_DIM = D_MODEL // A_GROUPS
N_HEADS = 8
N_KV_HEADS = 4
Q_PER_KV = N_HEADS // N_KV_HEADS
HEAD_DIM = D_MODEL // N_HEADS
KV_DIM = N_KV_HEADS * HEAD_DIM
N_IDX_HEADS = 8
IDX_DIM = 64
TOPK_MAX = 256
PAGE_SIZE = 128
LN_EPS = 1e-5

LANES = 128
SUBLANES = 8
BF16_SUBLANES = 16
VMEM_LIMIT_BYTES = 56 * 1024 * 1024

TOKEN_TILE = 512
Q_TILE = 128
KEY_CHUNK = 512
SAMPLE_ROWS = 64
SAMPLE_PAGES = 16
NEW_ROWS = 128

F32 = jnp.float32
BF16 = jnp.bfloat16
NEG_INF = float("-inf")
F32_LOWEST = float(jnp.finfo(jnp.float32).min)
M_INIT = -1e30
INT32_MIN = -(2 ** 31)


def _dot(a, b):
    return jnp.dot(a, b, preferred_element_type=F32)


def _dot_nt(a, b):
    return lax.dot_general(a, b, (((1,), (1,)), ((), ())), preferred_element_type=F32)


def _gelu(x):
    c = math.sqrt(2.0 / math.pi)
    return x * (0.5 * (1.0 + jnp.tanh(c * (x + 0.044715 * (x * x * x)))))


def _sigmoid(x):
    return 1.0 / (1.0 + jnp.exp(-x))


def _layer_norm(x, g, b):
    mu = jnp.mean(x, axis=-1, keepdims=True)
    xc = x - mu
    var = jnp.mean(xc * xc, axis=-1, keepdims=True)
    return xc * lax.rsqrt(var + LN_EPS) * g + b


def _key_to_float(u):
    c = u ^ jnp.int32(INT32_MIN)
    bits = c ^ ((c >> 31) & jnp.int32(0x7FFFFFFF))
    return lax.bitcast_convert_type(bits, F32)


def _resident(shape):
    nd = len(shape)
    return pl.BlockSpec(shape, lambda *_: (0,) * nd, pipeline_mode=pl.Buffered(1))


def _input_proj_kernel(x_ref, wuv_ref, wq_ref, wkv_ref, wqi_ref, wkw_ref, wg_ref, lng_ref, lnb_ref,
                       wmix_ref, bmix_ref, wpa_ref,
                       a_ref, sgb_ref, q_ref, k_ref, v_ref, kb_ref, vb_ref, qi_ref, kw_ref, *rest,
                       chunk_len, emit_vnorm):
    if emit_vnorm:
        vn_ref, aout_ref = rest
    else:
        (aout_ref,) = rest
    tm = x_ref.shape[0]
    xb = x_ref[...].astype(BF16)

    zuv = _dot(xb, wuv_ref[...])
    u = _gelu(zuv[:, :D_MODEL])
    vn = _layer_norm(_gelu(zuv[:, D_MODEL:]), lng_ref[...], lnb_ref[...])
    if emit_vnorm:
        vn_ref[...] = vn
    vnb = vn.astype(BF16)

    row = lax.broadcasted_iota(jnp.int32, (CHUNK, CHUNK), 0)
    col = lax.broadcasted_iota(jnp.int32, (CHUNK, CHUNK), 1)
    same_chunk = (row & ~(chunk_len - 1)) == (col & ~(chunk_len - 1))
    mix_mask = (col <= row) & same_chunk
    for g in range(A_GROUPS):
        gs = slice(g * A_GROUP_DIM, (g + 1) * A_GROUP_DIM)
        wm = jnp.where(mix_mask, wmix_ref[g], jnp.zeros((), BF16))
        for c in range(tm // CHUNK):
            rs = slice(c * CHUNK, (c + 1) * CHUNK)
            s = _dot(wm, vnb[rs, gs]) + bmix_ref[:, gs]
            aout_ref[rs, gs] = (u[rs, gs] * s).astype(BF16)

    pa = _dot(aout_ref[...], wpa_ref[...])
    gates = _dot(xb, wg_ref[...])
    a_ref[...] = (_sigmoid(gates[:, :D_MODEL]) * pa).astype(BF16)
    sgb_ref[...] = _sigmoid(gates[:, D_MODEL:]).astype(BF16)

    q_ref[...] = (_dot(xb, wq_ref[...]) * (HEAD_DIM ** -0.5)).astype(BF16)
    kv = _dot(xb, wkv_ref[...])
    k_ref[...] = kv[:, :KV_DIM]
    v_ref[...] = kv[:, KV_DIM:]
    kb_ref[...] = kv[:, :KV_DIM].astype(BF16)
    vb_ref[...] = kv[:, KV_DIM:].astype(BF16)
    qi_ref[...] = _dot(xb, wqi_ref[...]).astype(BF16)
    kw_ref[...] = _dot(xb, wkw_ref[...])


def _input_proj(x, wts, *, chunk_len, emit_vnorm):
    t = x.shape[0]
    tm = min(TOKEN_TILE, t)
    assert t % tm == 0 and tm % CHUNK == 0
    row_spec = lambda n: pl.BlockSpec((tm, n), lambda i: (i, 0))
    out_shapes = [
        jax.ShapeDtypeStruct((t, D_MODEL), BF16),
        jax.ShapeDtypeStruct((t, D_MODEL), BF16),
        jax.ShapeDtypeStruct((t, D_MODEL), BF16),
        jax.ShapeDtypeStruct((t, KV_DIM), F32),
        jax.ShapeDtypeStruct((t, KV_DIM), F32),
        jax.ShapeDtypeStruct((t, KV_DIM), BF16),
        jax.ShapeDtypeStruct((t, KV_DIM), BF16),
        jax.ShapeDtypeStruct((t, N_IDX_HEADS * IDX_DIM), BF16),
        jax.ShapeDtypeStruct((t, LANES), F32),
    ]
    out_specs = [row_spec(s.shape[1]) for s in out_shapes]
    if emit_vnorm:
        out_shapes.append(jax.ShapeDtypeStruct((t, D_MODEL), F32))
        out_specs.append(row_spec(D_MODEL))
    weights = (wts["wuv"], wts["wq"], wts["wkv"], wts["wqi"], wts["wkw"], wts["wg"], wts["ln_v_g"],
               wts["ln_v_b"], wts["wmix"], wts["bmix"], wts["wpa"])
    return pl.pallas_call(
        functools.partial(_input_proj_kernel, chunk_len=chunk_len, emit_vnorm=emit_vnorm),
        grid=(t // tm,),
        in_specs=[row_spec(D_MODEL)] + [_resident(w.shape) for w in weights],
        out_specs=out_specs,
        out_shape=out_shapes,
        scratch_shapes=[pltpu.VMEM((tm, D_MODEL), BF16)],
        compiler_params=pltpu.CompilerParams(dimension_semantics=("arbitrary",),
                                             vmem_limit_bytes=VMEM_LIMIT_BYTES),
        name="input_proj_sample" if emit_vnorm else "input_proj_prompt",
    )(x, *weights)


def _merge_ffn_kernel(x_ref, a_ref, sgb_ref, bo_ref, wpb_ref, wo_ref, ln1g_ref, ln1b_ref, wgate_ref,
                      wup_ref, wdown_ref, ln2g_ref, ln2b_ref, y_ref, *, alpha):
    pb = _dot(bo_ref[...], wpb_ref[...])
    merged = a_ref[...].astype(F32) + sgb_ref[...].astype(F32) * pb
    mix = _dot(merged.astype(BF16), wo_ref[...])
    x1 = _layer_norm(alpha * x_ref[...] + mix, ln1g_ref[...], ln1b_ref[...])
    x1b = x1.astype(BF16)
    hg = _dot(x1b, wgate_ref[...])
    hu = _dot(x1b, wup_ref[...])
    h = (hg * _sigmoid(hg)) * hu
    f = _dot(h.astype(BF16), wdown_ref[...])
    y_ref[...] = _layer_norm(alpha * x1 + f, ln2g_ref[...], ln2b_ref[...])


def _merge_ffn(x, a, sgb, bo, wts, *, alpha, name):
    t = x.shape[0]
    tm = min(TOKEN_TILE, t)
    assert t % tm == 0
    row_spec = pl.BlockSpec((tm, D_MODEL), lambda i: (i, 0))
    weights = (wts["wpb"], wts["wo"], wts["ln1_g"], wts["ln1_b"], wts["wgate"], wts["wup"], wts["wdown"],
               wts["ln2_g"], wts["ln2_b"])
    return pl.pallas_call(
        functools.partial(_merge_ffn_kernel, alpha=alpha),
        grid=(t // tm,),
        in_specs=[row_spec] * 4 + [_resident(w.shape) for w in weights],
        out_specs=row_spec,
        out_shape=jax.ShapeDtypeStruct((t, D_MODEL), F32),
        compiler_params=pltpu.CompilerParams(dimension_semantics=("arbitrary",),
                                             vmem_limit_bytes=VMEM_LIMIT_BYTES),
        name=name,
    )(x, a, sgb, bo, *weights)


def _lane_partial(x):
    acc = x[:, :LANES]
    for i in range(1, x.shape[1] // LANES):
        acc = acc + x[:, i * LANES:(i + 1) * LANES]
    return acc


def _kth_largest_key(count_ge, rows, topk):
    def bit_body(i, u):
        cand = u | lax.shift_left(jnp.int32(1), 31 - i)
        cnt = count_ge(_key_to_float(cand))
        return jnp.where(cnt >= topk, cand, u)
    return lax.fori_loop(0, 32, bit_body, jnp.zeros((rows, 1), jnp.int32))


def _select_chunk(s, thr, need, carry, tri):
    gt = s > thr
    eq = s == thr
    eqf = eq.astype(F32)
    inc = _dot(eqf.astype(BF16), tri)
    before = carry + inc - eqf
    sel = gt | (eq & (before < need))
    bias = jnp.where(sel, 0.0, NEG_INF)
    return bias, carry + inc[:, inc.shape[1] - 1:]


def _prompt_attn_kernel(q_ref, qi_ref, kwq_ref, kb_ref, vb_ref, kw_ref, tri_ref, o_ref,
                        sc_ref, m_ref, acc_ref, *, topk):
    tq = q_ref.shape[1]
    ck = KEY_CHUNK
    j = pl.program_id(1)
    n_chunks = lax.shift_right_logical((j + 1) * tq + (ck - 1), int(math.log2(ck)))
    t = j * tq + lax.broadcasted_iota(jnp.int32, (tq, 1), 0)

    qi = qi_ref[0]
    wv = (kwq_ref[0][:, IDX_DIM:IDX_DIM + N_IDX_HEADS] * (N_IDX_HEADS ** -0.5)) * (IDX_DIM ** -0.5)
    qis = [qi[:, h * IDX_DIM:(h + 1) * IDX_DIM] for h in range(N_IDX_HEADS)]
    wcols = [wv[:, h:h + 1] for h in range(N_IDX_HEADS)]

    def score_chunk(c, carry):
        start = pl.multiple_of(c * ck, ck)
        kc = kw_ref[0, pl.ds(start, ck), :][:, :IDX_DIM].astype(BF16)
        acc = jnp.zeros((tq, ck), F32)
        for h in range(N_IDX_HEADS):
            acc = acc + jnp.maximum(_dot_nt(qis[h], kc), 0.0) * wcols[h]
        kpos = start + lax.broadcasted_iota(jnp.int32, (1, ck), 1)
        sc_ref[:, pl.ds(start, ck)] = jnp.where(kpos <= t, acc, NEG_INF)
        return carry

    lax.fori_loop(0, n_chunks, score_chunk, 0)

    def count(pred):
        def body(c, cnt):
            s = sc_ref[:, pl.ds(pl.multiple_of(c * ck, ck), ck)]
            return cnt + _lane_partial(pred(s).astype(F32))
        part = lax.fori_loop(0, n_chunks, body, jnp.zeros((tq, LANES), F32))
        return jnp.sum(part, axis=-1, keepdims=True)

    u = _kth_largest_key(lambda cand: count(lambda s: s >= cand), tq, topk)
    thr = jnp.where(t < topk, F32_LOWEST, _key_to_float(u))
    need = topk - count(lambda s: s > thr)

    def select_chunk(c, carry):
        cs = pl.ds(pl.multiple_of(c * ck, ck), ck)
        bias, carry = _select_chunk(sc_ref[:, cs], thr, need, carry, tri_ref[...])
        sc_ref[:, cs] = bias
        return carry

    lax.fori_loop(0, n_chunks, select_chunk, jnp.zeros((tq, 1), F32))

    q = q_ref[0]
    ones = jnp.ones((ck, HEAD_DIM), BF16)
    for g in range(N_KV_HEADS):
        h0 = g * Q_PER_KV
        q2 = jnp.concatenate([q[:, (h0 + i) * HEAD_DIM:(h0 + i + 1) * HEAD_DIM] for i in range(Q_PER_KV)],
                             axis=0)
        m_ref[...] = jnp.full(m_ref.shape, M_INIT, F32)
        acc_ref[...] = jnp.zeros(acc_ref.shape, F32)
        gs = slice(g * HEAD_DIM, (g + 1) * HEAD_DIM)

        def attend_chunk(c, carry, q2=q2, gs=gs):
            cs = pl.ds(pl.multiple_of(c * ck, ck), ck)
            bias = sc_ref[:, cs]
            s = _dot_nt(q2, kb_ref[0, cs, gs]) + jnp.concatenate([bias] * Q_PER_KV, axis=0)
            m_old = m_ref[...][:, :1]
            m_new = jnp.maximum(m_old, jnp.max(s, axis=-1, keepdims=True))
            p = jnp.exp(s - m_new).astype(BF16)
            v1 = jnp.concatenate([vb_ref[0, cs, gs], ones], axis=1)
            acc_ref[...] = jnp.exp(m_old - m_new) * acc_ref[...] + _dot(p, v1)
            m_ref[...] = jnp.broadcast_to(m_new, m_ref.shape)
            return carry

        lax.fori_loop(0, n_chunks, attend_chunk, 0)
        acc = acc_ref[...]
        o = acc[:, :HEAD_DIM] / acc[:, HEAD_DIM:]
        for i in range(Q_PER_KV):
            o_ref[0, :, (h0 + i) * HEAD_DIM:(h0 + i + 1) * HEAD_DIM] = o[i * tq:(i + 1) * tq].astype(BF16)


def _prompt_attn(q, qi, kw, kb, vb, tri):
    b, s, _ = q.shape
    tq = Q_TILE
    assert s % KEY_CHUNK == 0 and s % tq == 0
    topk = min(TOPK_MAX, s // 4)
    blk = lambda n: pl.BlockSpec((1, tq, n), lambda i, j: (i, j, 0))
    seq = lambda n: pl.BlockSpec((1, s, n), lambda i, j: (i, 0, 0))
    return pl.pallas_call(
        functools.partial(_prompt_attn_kernel, topk=topk),
        grid=(b, s // tq),
        in_specs=[blk(D_MODEL), blk(N_IDX_HEADS * IDX_DIM), blk(LANES), seq(KV_DIM), seq(KV_DIM), seq(LANES),
                  _resident(tri.shape)],
        out_specs=blk(D_MODEL),
        out_shape=jax.ShapeDtypeStruct((b, s, D_MODEL), BF16),
        scratch_shapes=[pltpu.VMEM((tq, s), F32),
                        pltpu.VMEM((Q_PER_KV * tq, LANES), F32),
                        pltpu.VMEM((Q_PER_KV * tq, 2 * HEAD_DIM), F32)],
        compiler_params=pltpu.CompilerParams(dimension_semantics=("arbitrary", "arbitrary"),
                                             vmem_limit_bytes=VMEM_LIMIT_BYTES),
        name="prompt_attn",
    )(q, qi, kw, kb, vb, kw, tri)


def _sample_score_kernel(pt_ref, qi_ref, w_ref, knew_ref, *refs, n_pages, t_new):
    del pt_ref
    page_refs = refs[:n_pages]
    sp_ref, sn_ref = refs[n_pages:]
    qi = qi_ref[0]
    w = (w_ref[0] * (N_IDX_HEADS ** -0.5)) * (IDX_DIM ** -0.5)

    def head_sum(keys_bf16):
        r = jnp.maximum(_dot_nt(qi, keys_bf16), 0.0) * w
        return [jnp.sum(r[i * N_IDX_HEADS:(i + 1) * N_IDX_HEADS], axis=0, keepdims=True)
                for i in range(t_new)]

    group = 4
    for p0 in range(0, n_pages, group):
        keys = jnp.concatenate([page_refs[p][0] for p in range(p0, p0 + group)], axis=0).astype(BF16)
        for i, r in enumerate(head_sum(keys)):
            sp_ref[0, i:i + 1, p0 * PAGE_SIZE:(p0 + group) * PAGE_SIZE] = r

    knew = jnp.concatenate([knew_ref[0][:, :IDX_DIM],
                            jnp.zeros((LANES - knew_ref.shape[1], IDX_DIM), F32)], axis=0).astype(BF16)
    kpos = lax.broadcasted_iota(jnp.int32, (1, LANES), 1)
    for i, r in enumerate(head_sum(knew)):
        sn_ref[0, i:i + 1, :] = jnp.where(kpos <= i, r, NEG_INF)


def _sample_scores(page_table, qi32, w32, knew, cache_kidx):
    db, n_pages = page_table.shape
    t_new = qi32.shape[1] // N_IDX_HEADS
    assert n_pages % 4 == 0
    page_spec = lambda p: pl.BlockSpec((1, PAGE_SIZE, IDX_DIM), lambda b, pt: (pt[b, p], 0, 0))
    grid_spec = pltpu.PrefetchScalarGridSpec(
        num_scalar_prefetch=1,
        grid=(db,),
        in_specs=[pl.BlockSpec((1,) + qi32.shape[1:], lambda b, pt: (b, 0, 0)),
                  pl.BlockSpec((1,) + w32.shape[1:], lambda b, pt: (b, 0, 0)),
                  pl.BlockSpec((1,) + knew.shape[1:], lambda b, pt: (b, 0, 0))]
                 + [page_spec(p) for p in range(n_pages)],
        out_specs=[pl.BlockSpec((1, t_new, n_pages * PAGE_SIZE), lambda b, pt: (b, 0, 0)),
                   pl.BlockSpec((1, t_new, LANES), lambda b, pt: (b, 0, 0))],
    )
    return pl.pallas_call(
        functools.partial(_sample_score_kernel, n_pages=n_pages, t_new=t_new),
        grid_spec=grid_spec,
        out_shape=[jax.ShapeDtypeStruct((db, t_new, n_pages * PAGE_SIZE), F32),
                   jax.ShapeDtypeStruct((db, t_new, LANES), F32)],
        compiler_params=pltpu.CompilerParams(dimension_semantics=("arbitrary",),
                                             vmem_limit_bytes=VMEM_LIMIT_BYTES),
        name="sample_scores",
    )(page_table, qi32, w32, knew, *([cache_kidx] * n_pages))


def _sample_select_kernel(sp_ref, sn_ref, tri_ref, bp_ref, bn_ref, *, topk):
    rows, past = sp_ref.shape
    ck = KEY_CHUNK
    n_chunks = past // ck

    def count(pred):
        def body(c, cnt):
            s = sp_ref[:, pl.ds(pl.multiple_of(c * ck, ck), ck)]
            return cnt + _lane_partial(pred(s).astype(F32))
        part = lax.fori_loop(0, n_chunks, body, pred(sn_ref[...]).astype(F32))
        return jnp.sum(part, axis=-1, keepdims=True)

    u = _kth_largest_key(lambda cand: count(lambda s: s >= cand), rows, topk)
    thr = _key_to_float(u)
    need = topk - count(lambda s: s > thr)

    def select_chunk(c, carry):
        cs = pl.ds(pl.multiple_of(c * ck, ck), ck)
        bias, carry = _select_chunk(sp_ref[:, cs], thr, need, carry, tri_ref[...])
        bp_ref[:, cs] = bias
        return carry

    carry = lax.fori_loop(0, n_chunks, select_chunk, jnp.zeros((rows, 1), F32))
    bias, _ = _select_chunk(sn_ref[...], thr, need, carry, tri_ref[:LANES, :LANES])
    bn_ref[...] = bias


def _sample_select(sp, sn, tri, *, topk):
    n, past = sp.shape
    rows = min(SAMPLE_ROWS, n)
    assert n % rows == 0 and past % KEY_CHUNK == 0
    return pl.pallas_call(
        functools.partial(_sample_select_kernel, topk=topk),
        grid=(n // rows,),
        in_specs=[pl.BlockSpec((rows, past), lambda i: (i, 0)),
                  pl.BlockSpec((rows, LANES), lambda i: (i, 0)),
                  _resident(tri.shape)],
        out_specs=[pl.BlockSpec((rows, past), lambda i: (i, 0)),
                   pl.BlockSpec((rows, LANES), lambda i: (i, 0))],
        out_shape=[jax.ShapeDtypeStruct((n, past), F32), jax.ShapeDtypeStruct((n, LANES), F32)],
        compiler_params=pltpu.CompilerParams(dimension_semantics=("arbitrary",),
                                             vmem_limit_bytes=VMEM_LIMIT_BYTES),
        name="sample_select",
    )(sp, sn, tri)


def _sample_attn_kernel(pt_ref, q_ref, bp_ref, bn_ref, knew_ref, vnew_ref, *refs, n_pages, t_new):
    del pt_ref
    k_refs = refs[:n_pages]
    v_refs = refs[n_pages:2 * n_pages]
    o_ref, m_ref, l_ref, acc_ref = refs[2 * n_pages:]
    g = pl.program_id(1)
    q = q_ref[0]

    @pl.when(g == 0)
    def _():
        m_ref[...] = jnp.full(m_ref.shape, M_INIT, F32)
        l_ref[...] = jnp.zeros(l_ref.shape, F32)
        acc_ref[...] = jnp.zeros(acc_ref.shape, F32)

    def row_bias(b):
        return jnp.concatenate([jnp.broadcast_to(b[i:i + 1], (N_HEADS, b.shape[1])) for i in range(t_new)],
                               axis=0)

    def update(keys, vals, bias):
        s = _dot_nt(q, keys) + bias
        m_old = m_ref[...][:, :1]
        m_new = jnp.maximum(m_old, jnp.max(s, axis=-1, keepdims=True))
        p = jnp.exp(s - m_new)
        alpha = jnp.exp(m_old - m_new)
        l_ref[...] = alpha * l_ref[...] + jnp.sum(p, axis=-1, keepdims=True)
        acc_ref[...] = alpha * acc_ref[...] + _dot(p.astype(BF16), vals)
        m_ref[...] = jnp.broadcast_to(m_new, m_ref.shape)

    keys = jnp.concatenate([r[0] for r in k_refs], axis=0).astype(BF16)
    vals = jnp.concatenate([r[0] for r in v_refs], axis=0).astype(BF16)
    update(keys, vals, row_bias(bp_ref[0]))

    @pl.when(g == pl.num_programs(1) - 1)
    def _():
        update(knew_ref[0], vnew_ref[0], row_bias(bn_ref[0]))
        o = acc_ref[...] / l_ref[...][:, :1]
        kv_head = lax.shift_right_logical(
            lax.broadcasted_iota(jnp.int32, (o.shape[0], 1), 0) & (N_HEADS - 1), int(math.log2(Q_PER_KV)))
        out = jnp.zeros((o.shape[0], HEAD_DIM), F32)
        for k in range(N_KV_HEADS):
            out = out + jnp.where(kv_head == k, o[:, k * HEAD_DIM:(k + 1) * HEAD_DIM], 0.0)
        o_ref[0] = out.astype(BF16)


def _sample_attn(page_table, q32, bias_past, bias_new, knew, vnew, cache_k, cache_v):
    db, n_pages = page_table.shape
    rows = q32.shape[1]
    t_new = rows // N_HEADS
    pages = min(SAMPLE_PAGES, n_pages)
    assert n_pages % pages == 0
    k_spec = lambda p: pl.BlockSpec((1, PAGE_SIZE, KV_DIM), lambda b, g, pt: (pt[b, g * pages + p], 0, 0))
    per_seq = lambda shape: pl.BlockSpec((1,) + shape[1:], lambda b, g, pt: (b, 0, 0))
    grid_spec = pltpu.PrefetchScalarGridSpec(
        num_scalar_prefetch=1,
        grid=(db, n_pages // pages),
        in_specs=[per_seq(q32.shape),
                  pl.BlockSpec((1, t_new, pages * PAGE_SIZE), lambda b, g, pt: (b, 0, g)),
                  per_seq(bias_new.shape), per_seq(knew.shape), per_seq(vnew.shape)]
                 + [k_spec(p) for p in range(pages)] * 2,
        out_specs=pl.BlockSpec((1, rows, HEAD_DIM), lambda b, g, pt: (b, 0, 0)),
        scratch_shapes=[pltpu.VMEM((rows, LANES), F32), pltpu.VMEM((rows, LANES), F32),
                        pltpu.VMEM((rows, KV_DIM), F32)],
    )
    return pl.pallas_call(
        functools.partial(_sample_attn_kernel, n_pages=pages, t_new=t_new),
        grid_spec=grid_spec,
        out_shape=jax.ShapeDtypeStruct((db, rows, HEAD_DIM), BF16),
        compiler_params=pltpu.CompilerParams(dimension_semantics=("arbitrary", "arbitrary"),
                                             vmem_limit_bytes=VMEM_LIMIT_BYTES),
        name="sample_attn",
    )(page_table, q32, bias_past, bias_new, knew, vnew, *([cache_k] * pages), *([cache_v] * pages))


def _sample_mixer_b(page_table, q, qi, kw, kb, vb, cache_k, cache_v, cache_kidx, tri):
    db, n_pages = page_table.shape
    n_pool = cache_k.shape[0]
    t_new = q.shape[0] // db
    past = n_pages * PAGE_SIZE
    rows = t_new * N_IDX_HEADS
    qi32 = qi.reshape(db, rows, IDX_DIM)
    w32 = kw[:, IDX_DIM:IDX_DIM + N_IDX_HEADS].reshape(db, rows, 1)
    knew_idx = jnp.pad(kw.reshape(db, t_new, LANES), ((0, 0), (0, SUBLANES - t_new), (0, 0)))
    sp, sn = _sample_scores(page_table, qi32, w32, knew_idx, cache_kidx.reshape(n_pool, PAGE_SIZE, IDX_DIM))
    bias_p, bias_n = _sample_select(sp.reshape(db * t_new, past), sn.reshape(db * t_new, LANES), tri,
                                    topk=min(TOPK_MAX, (past + t_new) // 4))
    q5 = q.reshape(db, t_new, N_KV_HEADS, Q_PER_KV, 1, HEAD_DIM)
    onehot = jnp.eye(N_KV_HEADS, dtype=BF16).reshape(1, 1, N_KV_HEADS, 1, N_KV_HEADS, 1)
    q32 = (q5 * onehot).reshape(db, t_new * N_HEADS, KV_DIM)
    pad_new = lambda arr: jnp.pad(arr.reshape(db, t_new, KV_DIM), ((0, 0), (0, NEW_ROWS - t_new), (0, 0)))
    bo = _sample_attn(page_table, q32, bias_p.reshape(db, t_new, past), bias_n.reshape(db, t_new, LANES),
                      pad_new(kb), pad_new(vb), cache_k.reshape(n_pool, PAGE_SIZE, KV_DIM),
                      cache_v.reshape(n_pool, PAGE_SIZE, KV_DIM))
    return bo.reshape(db * t_new, D_MODEL)


def _prepare_weights(layer, w_in, ln_v_g, ln_v_b, w_s, b_s, w_pa, w_pb, w_o, ln1_g, ln1_b, w_gate, w_up,
                     w_down, ln2_g, ln2_b, t_new):
    w = w_in[layer]
    o_q = 2 * D_MODEL
    o_kv = o_q + D_MODEL
    o_qi = o_kv + 2 * KV_DIM
    o_ki = o_qi + N_IDX_HEADS * IDX_DIM
    o_g = o_ki + IDX_DIM + N_IDX_HEADS
    row = lambda v: v[layer].reshape(1, -1)
    reps = CHUNK // t_new
    shared = dict(
        wuv=w[:, :o_q].astype(BF16), wq=w[:, o_q:o_kv].astype(BF16), wkv=w[:, o_kv:o_qi].astype(BF16),
        wqi=w[:, o_qi:o_ki].astype(BF16),
        wkw=jnp.pad(w[:, o_ki:o_g], ((0, 0), (0, LANES - (o_g - o_ki)))).astype(BF16),
        wg=w[:, o_g:].astype(BF16),
        ln_v_g=row(ln_v_g), ln_v_b=row(ln_v_b), wpa=w_pa[layer].astype(BF16),
        wpb=w_pb[layer].astype(BF16), wo=w_o[layer].astype(BF16), ln1_g=row(ln1_g), ln1_b=row(ln1_b),
        wgate=w_gate[layer].astype(BF16), wup=w_up[layer].astype(BF16), wdown=w_down[layer].astype(BF16),
        ln2_g=row(ln2_g), ln2_b=row(ln2_b))
    prompt = dict(shared, wmix=w_s[layer].astype(BF16),
                  bmix=jnp.repeat(b_s[layer].T, A_GROUP_DIM, axis=1))
    sample = dict(shared, wmix=jnp.tile(w_s[layer][:, :t_new, :t_new], (1, reps, reps)).astype(BF16),
                  bmix=jnp.repeat(jnp.tile(b_s[layer][:, :t_new].T, (reps, 1)), A_GROUP_DIM, axis=1))
    return prompt, sample


def kernel(x_prompt, x_sample, cache_k, cache_v, cache_kidx, page_table, w_in, ln_v_g, ln_v_b, w_s, b_s,
           w_pa, w_pb, w_o, ln1_g, ln1_b, w_gate, w_up, w_down, ln2_g, ln2_b):
    depth = w_in.shape[0]
    alpha = (2 * depth) ** 0.25
    b, s, _ = x_prompt.shape
    db, t_new, _ = x_sample.shape
    n_pages = page_table.shape[1]
    n_pool = cache_k.shape[1]
    past = n_pages * PAGE_SIZE
    assert t_new <= CHUNK and CHUNK % t_new == 0 and t_new <= NEW_ROWS
    tri = jnp.triu(jnp.ones((KEY_CHUNK, KEY_CHUNK), BF16))

    y_p = x_prompt.reshape(b * s, D_MODEL)
    y_s = x_sample.reshape(db * t_new, D_MODEL)
    outs = [[] for _ in range(7)]
    for layer in range(depth):
        wp, ws = _prepare_weights(layer, w_in, ln_v_g, ln_v_b, w_s, b_s, w_pa, w_pb, w_o, ln1_g, ln1_b,
                                  w_gate, w_up, w_down, ln2_g, ln2_b, t_new)

        a, sgb, q, k_p, v_p, kb, vb, qi, kw = _input_proj(y_p, wp, chunk_len=CHUNK, emit_vnorm=False)
        seq = lambda arr: arr.reshape(b, s, arr.shape[-1])
        bo = _prompt_attn(seq(q), seq(qi), seq(kw), seq(kb), seq(vb), tri)
        y_p = _merge_ffn(y_p, a, sgb, bo.reshape(b * s, D_MODEL), wp, alpha=alpha, name="merge_ffn_prompt")
        outs[0].append(k_p.reshape(b, s, N_KV_HEADS, HEAD_DIM))
        outs[1].append(v_p.reshape(b, s, N_KV_HEADS, HEAD_DIM))
        outs[2].append(kw[:, :IDX_DIM].reshape(b, s, IDX_DIM))

        a, sgb, q, k_s, v_s, kb, vb, qi, kw, vn = _input_proj(y_s, ws, chunk_len=t_new, emit_vnorm=True)
        bo = _sample_mixer_b(page_table, q, qi, kw, kb, vb, cache_k[layer], cache_v[layer], cache_kidx[layer],
                             tri)
        y_s = _merge_ffn(y_s, a, sgb, bo, ws, alpha=alpha, name="merge_ffn_sample")
        outs[3].append(k_s.reshape(db, t_new, N_KV_HEADS, HEAD_DIM))
        outs[4].append(v_s.reshape(db, t_new, N_KV_HEADS, HEAD_DIM))
        outs[5].append(kw[:, :IDX_DIM].reshape(db, t_new, IDX_DIM))
        outs[6].append(vn.reshape(db, t_new, D_MODEL))

    return (y_p.reshape(b, s, D_MODEL), y_s.reshape(db, t_new, D_MODEL), *[jnp.stack(o) for o in outs])
```

```python
import functools
import math

import jax
import jax.numpy as jnp
from jax import lax
from jax.experimental import pallas as pl
from jax.experimental.pallas import tpu as pltpu

D_MODEL = 1024
CHUNK = 128
A_GROUPS = 8
A_GROUP_DIM = D_MODEL // A_GROUPS
N_HEADS = 8
N_KV_HEADS = 4
Q_PER_KV = N_HEADS // N_KV_HEADS
HEAD_DIM = D_MODEL // N_HEADS
KV_DIM = N_KV_HEADS * HEAD_DIM
N_IDX_HEADS = 8
IDX_DIM = 64
TOPK_MAX = 256
PAGE_SIZE = 128
LN_EPS = 1e-5

LANES = 128
SUBLANES = 8
BF16_SUBLANES = 16
VMEM_LIMIT_BYTES = 56 * 1024 * 1024

TOKEN_TILE = 512
Q_TILE = 128
KEY_CHUNK = 512
SAMPLE_ROWS = 64
SAMPLE_PAGES = 16
NEW_ROWS = 128

F32 = jnp.float32
BF16 = jnp.bfloat16
NEG_INF = float("-inf")
F32_LOWEST = float(jnp.finfo(jnp.float32).min)
M_INIT = -1e30
INT32_MIN = -(2 ** 31)


def _dot(a, b):
    return jnp.dot(a, b, preferred_element_type=F32)


def _dot_nt(a, b):
    return lax.dot_general(a, b, (((1,), (1,)), ((), ())), preferred_element_type=F32)


def _gelu(x):
    c = math.sqrt(2.0 / math.pi)
    return x * (0.5 * (1.0 + jnp.tanh(c * (x + 0.044715 * (x * x * x)))))


def _sigmoid(x):
    return 1.0 / (1.0 + jnp.exp(-x))


def _layer_norm(x, g, b):
    mu = jnp.mean(x, axis=-1, keepdims=True)
    xc = x - mu
    var = jnp.mean(xc * xc, axis=-1, keepdims=True)
    return xc * lax.rsqrt(var + LN_EPS) * g + b


def _key_to_float(u):
    c = u ^ jnp.int32(INT32_MIN)
    bits = c ^ ((c >> 31) & jnp.int32(0x7FFFFFFF))
    return lax.bitcast_convert_type(bits, F32)


def _resident(shape):
    nd = len(shape)
    return pl.BlockSpec(shape, lambda *_: (0,) * nd, pipeline_mode=pl.Buffered(1))


def _input_proj_kernel(x_ref, wuv_ref, wq_ref, wkv_ref, wqi_ref, wkw_ref, wg_ref, lng_ref, lnb_ref,
                       wmix_ref, bmix_ref, wpa_ref,
                       a_ref, sgb_ref, q_ref, k_ref, v_ref, kb_ref, vb_ref, qi_ref, kw_ref, *rest,
                       chunk_len, emit_vnorm):
    if emit_vnorm:
        vn_ref, aout_ref = rest
    else:
        (aout_ref,) = rest
    tm = x_ref.shape[0]
    xb = x_ref[...].astype(BF16)

    zuv = _dot(xb, wuv_ref[...])
    u = _gelu(zuv[:, :D_MODEL])
    vn = _layer_norm(_gelu(zuv[:, D_MODEL:]), lng_ref[...], lnb_ref[...])
    if emit_vnorm:
        vn_ref[...] = vn
    vnb = vn.astype(BF16)

    row = lax.broadcasted_iota(jnp.int32, (CHUNK, CHUNK), 0)
    col = lax.broadcasted_iota(jnp.int32, (CHUNK, CHUNK), 1)
    same_chunk = (row & ~(chunk_len - 1)) == (col & ~(chunk_len - 1))
    mix_mask = (col <= row) & same_chunk
    for g in range(A_GROUPS):
        gs = slice(g * A_GROUP_DIM, (g + 1) * A_GROUP_DIM)
        wm = jnp.where(mix_mask, wmix_ref[g], jnp.zeros((), BF16))
        for c in range(tm // CHUNK):
            rs = slice(c * CHUNK, (c + 1) * CHUNK)
            s = _dot(wm, vnb[rs, gs]) + bmix_ref[:, gs]
            aout_ref[rs, gs] = (u[rs, gs] * s).astype(BF16)

    pa = _dot(aout_ref[...], wpa_ref[...])
    gates = _dot(xb, wg_ref[...])
    a_ref[...] = (_sigmoid(gates[:, :D_MODEL]) * pa).astype(BF16)
    sgb_ref[...] = _sigmoid(gates[:, D_MODEL:]).astype(BF16)

    q_ref[...] = (_dot(xb, wq_ref[...]) * (HEAD_DIM ** -0.5)).astype(BF16)
    kv = _dot(xb, wkv_ref[...])
    for h in range(N_KV_HEADS):
        k_ref[pl.ds(h, tm, stride=N_KV_HEADS), :] = kv[:, h * HEAD_DIM:(h + 1) * HEAD_DIM]
        v_ref[pl.ds(h, tm, stride=N_KV_HEADS), :] = kv[:, KV_DIM + h * HEAD_DIM:KV_DIM + (h + 1) * HEAD_DIM]
    kb_ref[...] = kv[:, :KV_DIM].astype(BF16)
    vb_ref[...] = kv[:, KV_DIM:].astype(BF16)
    qi_ref[...] = _dot(xb, wqi_ref[...]).astype(BF16)
    kw_ref[...] = _dot(xb, wkw_ref[...])


def _input_proj(x, wts, *, chunk_len, emit_vnorm):
    t = x.shape[0]
    tm = min(TOKEN_TILE, t)
    assert t % tm == 0 and tm % CHUNK == 0
    row_spec = lambda n, rows_per_token=1: pl.BlockSpec((tm * rows_per_token, n), lambda i: (i, 0))
    out_shapes = [
        jax.ShapeDtypeStruct((t, D_MODEL), BF16),
        jax.ShapeDtypeStruct((t, D_MODEL), BF16),
        jax.ShapeDtypeStruct((t, D_MODEL), BF16),
        jax.ShapeDtypeStruct((t * N_KV_HEADS, HEAD_DIM), F32),
        jax.ShapeDtypeStruct((t * N_KV_HEADS, HEAD_DIM), F32),
        jax.ShapeDtypeStruct((t, KV_DIM), BF16),
        jax.ShapeDtypeStruct((t, KV_DIM), BF16),
        jax.ShapeDtypeStruct((t, N_IDX_HEADS * IDX_DIM), BF16),
        jax.ShapeDtypeStruct((t, LANES), F32),
    ]
    out_specs = [row_spec(s.shape[1], s.shape[0] // t) for s in out_shapes]
    if emit_vnorm:
        out_shapes.append(jax.ShapeDtypeStruct((t, D_MODEL), F32))
        out_specs.append(row_spec(D_MODEL))
    weights = (wts["wuv"], wts["wq"], wts["wkv"], wts["wqi"], wts["wkw"], wts["wg"], wts["ln_v_g"],
               wts["ln_v_b"], wts["wmix"], wts["bmix"], wts["wpa"])
    return pl.pallas_call(
        functools.partial(_input_proj_kernel, chunk_len=chunk_len, emit_vnorm=emit_vnorm),
        grid=(t // tm,),
        in_specs=[row_spec(D_MODEL)] + [_resident(w.shape) for w in weights],
        out_specs=out_specs,
        out_shape=out_shapes,
        scratch_shapes=[pltpu.VMEM((tm, D_MODEL), BF16)],
        compiler_params=pltpu.CompilerParams(dimension_semantics=("arbitrary",),
                                             vmem_limit_bytes=VMEM_LIMIT_BYTES),
        name="input_proj_sample" if emit_vnorm else "input_proj_prompt",
    )(x, *weights)


def _merge_ffn_kernel(x_ref, a_ref, sgb_ref, bo_ref, wpb_ref, wo_ref, ln1g_ref, ln1b_ref, wgate_ref,
                      wup_ref, wdown_ref, ln2g_ref, ln2b_ref, y_ref, *, alpha):
    pb = _dot(bo_ref[...], wpb_ref[...])
    merged = a_ref[...].astype(F32) + sgb_ref[...].astype(F32) * pb
    mix = _dot(merged.astype(BF16), wo_ref[...])
    x1 = _layer_norm(alpha * x_ref[...] + mix, ln1g_ref[...], ln1b_ref[...])
    x1b = x1.astype(BF16)
    hg = _dot(x1b, wgate_ref[...])
    hu = _dot(x1b, wup_ref[...])
    h = (hg * _sigmoid(hg)) * hu
    f = _dot(h.astype(BF16), wdown_ref[...])
    y_ref[...] = _layer_norm(alpha * x1 + f, ln2g_ref[...], ln2b_ref[...])


def _merge_ffn(x, a, sgb, bo, wts, *, alpha, name):
    t = x.shape[0]
    tm = min(TOKEN_TILE, t)
    assert t % tm == 0
    row_spec = pl.BlockSpec((tm, D_MODEL), lambda i: (i, 0))
    weights = (wts["wpb"], wts["wo"], wts["ln1_g"], wts["ln1_b"], wts["wgate"], wts["wup"], wts["wdown"],
               wts["ln2_g"], wts["ln2_b"])
    return pl.pallas_call(
        functools.partial(_merge_ffn_kernel, alpha=alpha),
        grid=(t // tm,),
        in_specs=[row_spec] * 4 + [_resident(w.shape) for w in weights],
        out_specs=row_spec,
        out_shape=jax.ShapeDtypeStruct((t, D_MODEL), F32),
        compiler_params=pltpu.CompilerParams(dimension_semantics=("arbitrary",),
                                             vmem_limit_bytes=VMEM_LIMIT_BYTES),
        name=name,
    )(x, a, sgb, bo, *weights)


def _lane_partial(x):
    acc = x[:, :LANES]
    for i in range(1, x.shape[1] // LANES):
        acc = acc + x[:, i * LANES:(i + 1) * LANES]
    return acc


def _kth_largest_key(count_ge, rows, topk):
    def bit_body(i, carry):
        u, cnt_u = carry
        cand = u | lax.shift_left(jnp.int32(1), 31 - i)
        cnt = count_ge(_key_to_float(cand))
        take = cnt >= topk
        return jnp.where(take, cand, u), jnp.where(take, cnt, cnt_u)
    return lax.fori_loop(0, 32, bit_body, (jnp.zeros((rows, 1), jnp.int32), jnp.zeros((rows, 1), F32)))


def _threshold_bias(s, thr):
    return jnp.where(s >= thr, 0.0, NEG_INF)


def _select_chunk(s, thr, need, carry, tri):
    gt = s > thr
    eq = s == thr
    eqf = eq.astype(F32)
    inc = _dot(eqf.astype(BF16), tri)
    before = carry + inc - eqf
    sel = gt | (eq & (before < need))
    bias = jnp.where(sel, 0.0, NEG_INF)
    return bias, carry + inc[:, inc.shape[1] - 1:]


def _prompt_attn_kernel(q_ref, qi_ref, kwq_ref, kb_ref, vb_ref, kw_ref, tri_ref, o_ref,
                        sc_ref, m_ref, acc_ref, *, topk):
    tq = q_ref.shape[1]
    ck = KEY_CHUNK
    j = pl.program_id(1)
    n_chunks = lax.shift_right_logical((j + 1) * tq + (ck - 1), int(math.log2(ck)))
    t = j * tq + lax.broadcasted_iota(jnp.int32, (tq, 1), 0)

    qi = qi_ref[0]
    wv = (kwq_ref[0][:, IDX_DIM:IDX_DIM + N_IDX_HEADS] * (N_IDX_HEADS ** -0.5)) * (IDX_DIM ** -0.5)
    qis = [qi[:, h * IDX_DIM:(h + 1) * IDX_DIM] for h in range(N_IDX_HEADS)]
    wcols = [wv[:, h:h + 1] for h in range(N_IDX_HEADS)]

    def score_chunk(c, carry):
        start = pl.multiple_of(c * ck, ck)
        kc = kw_ref[0, pl.ds(start, ck), :][:, :IDX_DIM].astype(BF16)
        acc = jnp.zeros((tq, ck), F32)
        for h in range(N_IDX_HEADS):
            acc = acc + jnp.maximum(_dot_nt(qis[h], kc), 0.0) * wcols[h]
        kpos = start + lax.broadcasted_iota(jnp.int32, (1, ck), 1)
        sc_ref[:, pl.ds(start, ck)] = jnp.where(kpos <= t, acc, NEG_INF)
        return carry

    lax.fori_loop(0, n_chunks, score_chunk, 0)

    def count(pred):
        def body(c, cnt):
            s = sc_ref[:, pl.ds(pl.multiple_of(c * ck, ck), ck)]
            return cnt + _lane_partial(pred(s).astype(F32))
        part = lax.fori_loop(0, n_chunks, body, jnp.zeros((tq, LANES), F32))
        return jnp.sum(part, axis=-1, keepdims=True)

    u, cnt_u = _kth_largest_key(lambda cand: count(lambda s: s >= cand), tq, topk)
    small = t < topk
    thr = jnp.where(small, F32_LOWEST, _key_to_float(u))
    has_ties = jnp.max(jnp.where(small, 0.0, cnt_u - topk)) > 0.0

    @pl.when(has_ties)
    def _():
        need = topk - count(lambda s: s > thr)

        def select_chunk(c, carry):
            cs = pl.ds(pl.multiple_of(c * ck, ck), ck)
            bias, carry = _select_chunk(sc_ref[:, cs], thr, need, carry, tri_ref[...])
            sc_ref[:, cs] = bias
            return carry

        lax.fori_loop(0, n_chunks, select_chunk, jnp.zeros((tq, 1), F32))

    @pl.when(jnp.logical_not(has_ties))
    def _():
        def select_chunk(c, carry):
            cs = pl.ds(pl.multiple_of(c * ck, ck), ck)
            sc_ref[:, cs] = _threshold_bias(sc_ref[:, cs], thr)
            return carry

        lax.fori_loop(0, n_chunks, select_chunk, 0)

    q = q_ref[0]
    ones = jnp.ones((ck, HEAD_DIM), BF16)
    q2s = [jnp.concatenate([q[:, (g * Q_PER_KV + i) * HEAD_DIM:(g * Q_PER_KV + i + 1) * HEAD_DIM]
                            for i in range(Q_PER_KV)], axis=0) for g in range(N_KV_HEADS)]
    m_ref[...] = jnp.full(m_ref.shape, M_INIT, F32)
    acc_ref[...] = jnp.zeros(acc_ref.shape, F32)

    def attend_chunk(c, carry):
        cs = pl.ds(pl.multiple_of(c * ck, ck), ck)
        bias = jnp.concatenate([sc_ref[:, cs]] * Q_PER_KV, axis=0)
        for g in range(N_KV_HEADS):
            gs = slice(g * HEAD_DIM, (g + 1) * HEAD_DIM)
            s = _dot_nt(q2s[g], kb_ref[0, cs, gs]) + bias
            m_old = m_ref[g][:, :1]
            m_new = jnp.maximum(m_old, jnp.max(s, axis=-1, keepdims=True))
            p = jnp.exp(s - m_new).astype(BF16)
            v1 = jnp.concatenate([vb_ref[0, cs, gs], ones], axis=1)
            acc_ref[g] = jnp.exp(m_old - m_new) * acc_ref[g] + _dot(p, v1)
            m_ref[g] = jnp.broadcast_to(m_new, m_ref.shape[1:])
        return carry

    lax.fori_loop(0, n_chunks, attend_chunk, 0)
    for g in range(N_KV_HEADS):
        acc = acc_ref[g]
        o = acc[:, :HEAD_DIM] / acc[:, HEAD_DIM:]
        for i in range(Q_PER_KV):
            h = g * Q_PER_KV + i
            o_ref[0, :, h * HEAD_DIM:(h + 1) * HEAD_DIM] = o[i * tq:(i + 1) * tq].astype(BF16)


def _prompt_attn(q, qi, kw, kb, vb, tri):
    b, s, _ = q.shape
    tq = Q_TILE
    assert s % KEY_CHUNK == 0 and s % tq == 0
    topk = min(TOPK_MAX, s // 4)
    blk = lambda n: pl.BlockSpec((1, tq, n), lambda i, j: (i, j, 0))
    seq = lambda n: pl.BlockSpec((1, s, n), lambda i, j: (i, 0, 0))
    return pl.pallas_call(
        functools.partial(_prompt_attn_kernel, topk=topk),
        grid=(b, s // tq),
        in_specs=[blk(D_MODEL), blk(N_IDX_HEADS * IDX_DIM), blk(LANES), seq(KV_DIM), seq(KV_DIM), seq(LANES),
                  _resident(tri.shape)],
        out_specs=blk(D_MODEL),
        out_shape=jax.ShapeDtypeStruct((b, s, D_MODEL), BF16),
        scratch_shapes=[pltpu.VMEM((tq, s), F32),
                        pltpu.VMEM((N_KV_HEADS, Q_PER_KV * tq, LANES), F32),
                        pltpu.VMEM((N_KV_HEADS, Q_PER_KV * tq, 2 * HEAD_DIM), F32)],
        compiler_params=pltpu.CompilerParams(dimension_semantics=("arbitrary", "arbitrary"),
                                             vmem_limit_bytes=VMEM_LIMIT_BYTES),
        name="prompt_attn",
    )(q, qi, kw, kb, vb, kw, tri)


def _sample_score_kernel(pt_ref, qi_ref, w_ref, knew_ref, *refs, n_pages, t_new):
    del pt_ref
    page_refs = refs[:n_pages]
    sp_ref, sn_ref = refs[n_pages:]
    qi = qi_ref[0]
    w = (w_ref[0] * (N_IDX_HEADS ** -0.5)) * (IDX_DIM ** -0.5)

    def head_sum(dots):
        r = jnp.maximum(dots, 0.0) * w
        return [jnp.sum(r[i * N_IDX_HEADS:(i + 1) * N_IDX_HEADS], axis=0, keepdims=True)
                for i in range(t_new)]

    group = 4
    for p0 in range(0, n_pages, group):
        keys_t = jnp.concatenate([page_refs[p][0] for p in range(p0, p0 + group)], axis=1).astype(BF16)
        for i, r in enumerate(head_sum(_dot(qi, keys_t))):
            sp_ref[0, i:i + 1, p0 * PAGE_SIZE:(p0 + group) * PAGE_SIZE] = r

    knew = jnp.concatenate([knew_ref[0][:, :IDX_DIM],
                            jnp.zeros((LANES - knew_ref.shape[1], IDX_DIM), F32)], axis=0).astype(BF16)
    kpos = lax.broadcasted_iota(jnp.int32, (1, LANES), 1)
    for i, r in enumerate(head_sum(_dot_nt(qi, knew))):
        sn_ref[0, i:i + 1, :] = jnp.where(kpos <= i, r, NEG_INF)


def _sample_scores(page_table, qi32, w32, knew, cache_kidx_t):
    db, n_pages = page_table.shape
    t_new = qi32.shape[1] // N_IDX_HEADS
    assert n_pages % 4 == 0
    page_spec = lambda p: pl.BlockSpec((1, IDX_DIM, PAGE_SIZE), lambda b, pt: (pt[b, p], 0, 0))
    grid_spec = pltpu.PrefetchScalarGridSpec(
        num_scalar_prefetch=1,
        grid=(db,),
        in_specs=[pl.BlockSpec((1,) + qi32.shape[1:], lambda b, pt: (b, 0, 0)),
                  pl.BlockSpec((1,) + w32.shape[1:], lambda b, pt: (b, 0, 0)),
                  pl.BlockSpec((1,) + knew.shape[1:], lambda b, pt: (b, 0, 0))]
                 + [page_spec(p) for p in range(n_pages)],
        out_specs=[pl.BlockSpec((1, t_new, n_pages * PAGE_SIZE), lambda b, pt: (b, 0, 0)),
                   pl.BlockSpec((1, t_new, LANES), lambda b, pt: (b, 0, 0))],
    )
    return pl.pallas_call(
        functools.partial(_sample_score_kernel, n_pages=n_pages, t_new=t_new),
        grid_spec=grid_spec,
        out_shape=[jax.ShapeDtypeStruct((db, t_new, n_pages * PAGE_SIZE), F32),
                   jax.ShapeDtypeStruct((db, t_new, LANES), F32)],
        compiler_params=pltpu.CompilerParams(dimension_semantics=("arbitrary",),
                                             vmem_limit_bytes=VMEM_LIMIT_BYTES),
        name="sample_scores",
    )(page_table, qi32, w32, knew, *([cache_kidx_t] * n_pages))


def _sample_select_kernel(sp_ref, sn_ref, tri_ref, bp_ref, bn_ref, *, topk):
    rows, past = sp_ref.shape
    ck = KEY_CHUNK
    n_chunks = past // ck

    def count(pred):
        def body(c, cnt):
            s = sp_ref[:, pl.ds(pl.multiple_of(c * ck, ck), ck)]
            return cnt + _lane_partial(pred(s).astype(F32))
        part = lax.fori_loop(0, n_chunks, body, pred(sn_ref[...]).astype(F32))
        return jnp.sum(part, axis=-1, keepdims=True)

    u, cnt_u = _kth_largest_key(lambda cand: count(lambda s: s >= cand), rows, topk)
    thr = _key_to_float(u)
    has_ties = jnp.max(cnt_u) > topk

    @pl.when(has_ties)
    def _():
        need = topk - count(lambda s: s > thr)

        def select_chunk(c, carry):
            cs = pl.ds(pl.multiple_of(c * ck, ck), ck)
            bias, carry = _select_chunk(sp_ref[:, cs], thr, need, carry, tri_ref[...])
            bp_ref[:, cs] = bias
            return carry

        carry = lax.fori_loop(0, n_chunks, select_chunk, jnp.zeros((rows, 1), F32))
        bias, _ = _select_chunk(sn_ref[...], thr, need, carry, tri_ref[:LANES, :LANES])
        bn_ref[...] = bias

    @pl.when(jnp.logical_not(has_ties))
    def _():
        def select_chunk(c, carry):
            cs = pl.ds(pl.multiple_of(c * ck, ck), ck)
            bp_ref[:, cs] = _threshold_bias(sp_ref[:, cs], thr)
            return carry

        lax.fori_loop(0, n_chunks, select_chunk, 0)
        bn_ref[...] = _threshold_bias(sn_ref[...], thr)


def _sample_select(sp, sn, tri, *, topk):
    n, past = sp.shape
    rows = min(SAMPLE_ROWS, n)
    assert n % rows == 0 and past % KEY_CHUNK == 0
    return pl.pallas_call(
        functools.partial(_sample_select_kernel, topk=topk),
        grid=(n // rows,),
        in_specs=[pl.BlockSpec((rows, past), lambda i: (i, 0)),
                  pl.BlockSpec((rows, LANES), lambda i: (i, 0)),
                  _resident(tri.shape)],
        out_specs=[pl.BlockSpec((rows, past), lambda i: (i, 0)),
                   pl.BlockSpec((rows, LANES), lambda i: (i, 0))],
        out_shape=[jax.ShapeDtypeStruct((n, past), F32), jax.ShapeDtypeStruct((n, LANES), F32)],
        compiler_params=pltpu.CompilerParams(dimension_semantics=("arbitrary",),
                                             vmem_limit_bytes=VMEM_LIMIT_BYTES),
        name="sample_select",
    )(sp, sn, tri)


def _sample_attn_kernel(pt_ref, q_ref, bp_ref, bn_ref, knew_ref, vnew_ref, *refs, n_pages, t_new):
    del pt_ref
    k_refs = refs[:n_pages]
    v_refs = refs[n_pages:2 * n_pages]
    o_ref, m_ref, l_ref, acc_ref = refs[2 * n_pages:]
    g = pl.program_id(1)
    q = q_ref[0]

    @pl.when(g == 0)
    def _():
        m_ref[...] = jnp.full(m_ref.shape, M_INIT, F32)
        l_ref[...] = jnp.zeros(l_ref.shape, F32)
        acc_ref[...] = jnp.zeros(acc_ref.shape, F32)

    def row_bias(b):
        return jnp.concatenate([jnp.broadcast_to(b[i:i + 1], (N_HEADS, b.shape[1])) for i in range(t_new)],
                               axis=0)

    heads = [slice(h * HEAD_DIM, (h + 1) * HEAD_DIM) for h in range(N_KV_HEADS)]

    def update(keys, vals, bias):
        s = bias
        for h in range(N_KV_HEADS):
            s = s + _dot_nt(q[:, heads[h]], keys[h])
        m_old = m_ref[...][:, :1]
        m_new = jnp.maximum(m_old, jnp.max(s, axis=-1, keepdims=True))
        p = jnp.exp(s - m_new)
        alpha = jnp.exp(m_old - m_new)
        pb = p.astype(BF16)
        l_ref[...] = alpha * l_ref[...] + jnp.sum(p, axis=-1, keepdims=True)
        acc_ref[...] = alpha * acc_ref[...] + jnp.concatenate([_dot(pb, vals[h]) for h in range(N_KV_HEADS)],
                                                              axis=1)
        m_ref[...] = jnp.broadcast_to(m_new, m_ref.shape)

    def page_head(page_refs, h):
        return jnp.concatenate([r[0, pl.ds(h, PAGE_SIZE, stride=N_KV_HEADS), :] for r in page_refs],
                               axis=0).astype(BF16)

    update([page_head(k_refs, h) for h in range(N_KV_HEADS)],
           [page_head(v_refs, h) for h in range(N_KV_HEADS)], row_bias(bp_ref[0]))

    @pl.when(g == pl.num_programs(1) - 1)
    def _():
        update([knew_ref[0][:, hs] for hs in heads], [vnew_ref[0][:, hs] for hs in heads], row_bias(bn_ref[0]))
        o = acc_ref[...] / l_ref[...][:, :1]
        kv_head = lax.shift_right_logical(
            lax.broadcasted_iota(jnp.int32, (o.shape[0], 1), 0) & (N_HEADS - 1), int(math.log2(Q_PER_KV)))
        out = jnp.zeros((o.shape[0], HEAD_DIM), F32)
        for k in range(N_KV_HEADS):
            out = out + jnp.where(kv_head == k, o[:, k * HEAD_DIM:(k + 1) * HEAD_DIM], 0.0)
        o_ref[0] = out.astype(BF16)


def _sample_attn(page_table, q32, bias_past, bias_new, knew, vnew, cache_k, cache_v):
    db, n_pages = page_table.shape
    rows = q32.shape[1]
    t_new = rows // N_HEADS
    pages = min(SAMPLE_PAGES, n_pages)
    assert n_pages % pages == 0
    k_spec = lambda p: pl.BlockSpec((1, PAGE_SIZE * N_KV_HEADS, HEAD_DIM),
                                    lambda b, g, pt: (pt[b, g * pages + p], 0, 0))
    per_seq = lambda shape: pl.BlockSpec((1,) + shape[1:], lambda b, g, pt: (b, 0, 0))
    grid_spec = pltpu.PrefetchScalarGridSpec(
        num_scalar_prefetch=1,
        grid=(db, n_pages // pages),
        in_specs=[per_seq(q32.shape),
                  pl.BlockSpec((1, t_new, pages * PAGE_SIZE), lambda b, g, pt: (b, 0, g)),
                  per_seq(bias_new.shape), per_seq(knew.shape), per_seq(vnew.shape)]
                 + [k_spec(p) for p in range(pages)] * 2,
        out_specs=pl.BlockSpec((1, rows, HEAD_DIM), lambda b, g, pt: (b, 0, 0)),
        scratch_shapes=[pltpu.VMEM((rows, LANES), F32), pltpu.VMEM((rows, LANES), F32),
                        pltpu.VMEM((rows, KV_DIM), F32)],
    )
    return pl.pallas_call(
        functools.partial(_sample_attn_kernel, n_pages=pages, t_new=t_new),
        grid_spec=grid_spec,
        out_shape=jax.ShapeDtypeStruct((db, rows, HEAD_DIM), BF16),
        compiler_params=pltpu.CompilerParams(dimension_semantics=("arbitrary", "arbitrary"),
                                             vmem_limit_bytes=VMEM_LIMIT_BYTES),
        name="sample_attn",
    )(page_table, q32, bias_past, bias_new, knew, vnew, *([cache_k] * pages), *([cache_v] * pages))


def _sample_mixer_b(page_table, q, qi, kw, kb, vb, cache_k, cache_v, cache_kidx, tri):
    db, n_pages = page_table.shape
    n_pool = cache_k.shape[0]
    t_new = q.shape[0] // db
    past = n_pages * PAGE_SIZE
    rows = t_new * N_IDX_HEADS
    qi32 = qi.reshape(db, rows, IDX_DIM)
    w32 = kw[:, IDX_DIM:IDX_DIM + N_IDX_HEADS].reshape(db, rows, 1)
    knew_idx = jnp.pad(kw.reshape(db, t_new, LANES), ((0, 0), (0, SUBLANES - t_new), (0, 0)))
    sp, sn = _sample_scores(page_table, qi32, w32, knew_idx, jnp.swapaxes(cache_kidx, 1, 2))
    bias_p, bias_n = _sample_select(sp.reshape(db * t_new, past), sn.reshape(db * t_new, LANES), tri,
                                    topk=min(TOPK_MAX, (past + t_new) // 4))
    q5 = q.reshape(db, t_new, N_KV_HEADS, Q_PER_KV, 1, HEAD_DIM)
    onehot = jnp.eye(N_KV_HEADS, dtype=BF16).reshape(1, 1, N_KV_HEADS, 1, N_KV_HEADS, 1)
    q32 = (q5 * onehot).reshape(db, t_new * N_HEADS, KV_DIM)
    pad_new = lambda arr: jnp.pad(arr.reshape(db, t_new, KV_DIM), ((0, 0), (0, NEW_ROWS - t_new), (0, 0)))
    bo = _sample_attn(page_table, q32, bias_p.reshape(db, t_new, past), bias_n.reshape(db, t_new, LANES),
                      pad_new(kb), pad_new(vb), cache_k.reshape(n_pool, PAGE_SIZE * N_KV_HEADS, HEAD_DIM),
                      cache_v.reshape(n_pool, PAGE_SIZE * N_KV_HEADS, HEAD_DIM))
    return bo.reshape(db * t_new, D_MODEL)


def _prepare_weights(layer, w_in, ln_v_g, ln_v_b, w_s, b_s, w_pa, w_pb, w_o, ln1_g, ln1_b, w_gate, w_up,
                     w_down, ln2_g, ln2_b, t_new):
    w = w_in[layer]
    o_q = 2 * D_MODEL
    o_kv = o_q + D_MODEL
    o_qi = o_kv + 2 * KV_DIM
    o_ki = o_qi + N_IDX_HEADS * IDX_DIM
    o_g = o_ki + IDX_DIM + N_IDX_HEADS
    row = lambda v: v[layer].reshape(1, -1)
    reps = CHUNK // t_new
    shared = dict(
        wuv=w[:, :o_q].astype(BF16), wq=w[:, o_q:o_kv].astype(BF16), wkv=w[:, o_kv:o_qi].astype(BF16),
        wqi=w[:, o_qi:o_ki].astype(BF16),
        wkw=jnp.pad(w[:, o_ki:o_g], ((0, 0), (0, LANES - (o_g - o_ki)))).astype(BF16),
        wg=w[:, o_g:].astype(BF16),
        ln_v_g=row(ln_v_g), ln_v_b=row(ln_v_b), wpa=w_pa[layer].astype(BF16),
        wpb=w_pb[layer].astype(BF16), wo=w_o[layer].astype(BF16), ln1_g=row(ln1_g), ln1_b=row(ln1_b),
        wgate=w_gate[layer].astype(BF16), wup=w_up[layer].astype(BF16), wdown=w_down[layer].astype(BF16),
        ln2_g=row(ln2_g), ln2_b=row(ln2_b))
    prompt = dict(shared, wmix=w_s[layer].astype(BF16),
                  bmix=jnp.repeat(b_s[layer].T, A_GROUP_DIM, axis=1))
    sample = dict(shared, wmix=jnp.tile(w_s[layer][:, :t_new, :t_new], (1, reps, reps)).astype(BF16),
                  bmix=jnp.repeat(jnp.tile(b_s[layer][:, :t_new].T, (reps, 1)), A_GROUP_DIM, axis=1))
    return prompt, sample


def kernel(x_prompt, x_sample, cache_k, cache_v, cache_kidx, page_table, w_in, ln_v_g, ln_v_b, w_s, b_s,
           w_pa, w_pb, w_o, ln1_g, ln1_b, w_gate, w_up, w_down, ln2_g, ln2_b):
    depth = w_in.shape[0]
    alpha = (2 * depth) ** 0.25
    b, s, _ = x_prompt.shape
    db, t_new, _ = x_sample.shape
    n_pages = page_table.shape[1]
    n_pool = cache_k.shape[1]
    past = n_pages * PAGE_SIZE
    assert t_new <= CHUNK and CHUNK % t_new == 0 and t_new <= NEW_ROWS
    tri = jnp.triu(jnp.ones((KEY_CHUNK, KEY_CHUNK), BF16))

    y_p = x_prompt.reshape(b * s, D_MODEL)
    y_s = x_sample.reshape(db * t_new, D_MODEL)
    outs = [[] for _ in range(7)]
    for layer in range(depth):
        wp, ws = _prepare_weights(layer, w_in, ln_v_g, ln_v_b, w_s, b_s, w_pa, w_pb, w_o, ln1_g, ln1_b,
                                  w_gate, w_up, w_down, ln2_g, ln2_b, t_new)

        a, sgb, q, k_p, v_p, kb, vb, qi, kw = _input_proj(y_p, wp, chunk_len=CHUNK, emit_vnorm=False)
        seq = lambda arr: arr.reshape(b, s, arr.shape[-1])
        bo = _prompt_attn(seq(q), seq(qi), seq(kw), seq(kb), seq(vb), tri)
        y_p = _merge_ffn(y_p, a, sgb, bo.reshape(b * s, D_MODEL), wp, alpha=alpha, name="merge_ffn_prompt")
        outs[0].append(k_p.reshape(b, s, N_KV_HEADS, HEAD_DIM))
        outs[1].append(v_p.reshape(b, s, N_KV_HEADS, HEAD_DIM))
        outs[2].append(kw[:, :IDX_DIM].reshape(b, s, IDX_DIM))

        a, sgb, q, k_s, v_s, kb, vb, qi, kw, vn = _input_proj(y_s, ws, chunk_len=t_new, emit_vnorm=True)
        bo = _sample_mixer_b(page_table, q, qi, kw, kb, vb, cache_k[layer], cache_v[layer], cache_kidx[layer],
                             tri)
        y_s = _merge_ffn(y_s, a, sgb, bo, ws, alpha=alpha, name="merge_ffn_sample")
        outs[3].append(k_s.reshape(db, t_new, N_KV_HEADS, HEAD_DIM))
        outs[4].append(v_s.reshape(db, t_new, N_KV_HEADS, HEAD_DIM))
        outs[5].append(kw[:, :IDX_DIM].reshape(db, t_new, IDX_DIM))
        outs[6].append(vn.reshape(db, t_new, D_MODEL))

    return (y_p.reshape(b, s, D_MODEL), y_s.reshape(db, t_new, D_MODEL), *[jnp.stack(o) for o in outs])
```

```python
import functools
import math

import jax
import jax.numpy as jnp
from jax import lax
from jax.experimental import pallas as pl
from jax.experimental.pallas import tpu as pltpu

D_MODEL = 1024
CHUNK = 128
A_GROUPS = 8
A_GROUP_DIM = D_MODEL // A_GROUPS
N_HEADS = 8
N_KV_HEADS = 4
Q_PER_KV = N_HEADS // N_KV_HEADS
HEAD_DIM = D_MODEL // N_HEADS
KV_DIM = N_KV_HEADS * HEAD_DIM
N_IDX_HEADS = 8
IDX_DIM = 64
TOPK_MAX = 256
PAGE_SIZE = 128
LN_EPS = 1e-5

LANES = 128
SUBLANES = 8
BF16_SUBLANES = 16
VMEM_LIMIT_BYTES = 56 * 1024 * 1024

TOKEN_TILE = 512
Q_TILE = 512
KEY_CHUNK = 512
SAMPLE_ROWS = 64
SAMPLE_PAGES = 16
NEW_ROWS = 128

F32 = jnp.float32
BF16 = jnp.bfloat16
NEG_INF = float("-inf")
F32_LOWEST = float(jnp.finfo(jnp.float32).min)
M_INIT = -1e30
LOG2_E = math.log2(math.e)
INT32_MIN = -(2 ** 31)


def _dot(a, b):
    return jnp.dot(a, b, preferred_element_type=F32)


def _dot_nt(a, b):
    return lax.dot_general(a, b, (((1,), (1,)), ((), ())), preferred_element_type=F32)


def _gelu(x):
    c = math.sqrt(2.0 / math.pi)
    return x * (0.5 * (1.0 + jnp.tanh(c * (x + 0.044715 * (x * x * x)))))


def _sigmoid(x):
    return 1.0 / (1.0 + jnp.exp(-x))


def _layer_norm(x, g, b):
    mu = jnp.mean(x, axis=-1, keepdims=True)
    xc = x - mu
    var = jnp.mean(xc * xc, axis=-1, keepdims=True)
    return xc * lax.rsqrt(var + LN_EPS) * g + b


def _key_to_float(u):
    c = u ^ jnp.int32(INT32_MIN)
    bits = c ^ ((c >> 31) & jnp.int32(0x7FFFFFFF))
    return lax.bitcast_convert_type(bits, F32)


def _resident(shape):
    nd = len(shape)
    return pl.BlockSpec(shape, lambda *_: (0,) * nd, pipeline_mode=pl.Buffered(1))


def _input_proj_kernel(x_ref, wuv_ref, wq_ref, wkv_ref, wqi_ref, wkw_ref, wg_ref, lng_ref, lnb_ref,
                       wmix_ref, bmix_ref, wpa_ref,
                       a_ref, sgb_ref, q_ref, k_ref, v_ref, kb_ref, vb_ref, qi_ref, kw_ref, *rest,
                       chunk_len, emit_vnorm, transpose_v, q_scale):
    if emit_vnorm:
        vn_ref, aout_ref = rest
    else:
        (aout_ref,) = rest
    tm = x_ref.shape[0]
    xb = x_ref[...].astype(BF16)

    zuv = _dot(xb, wuv_ref[...])
    u = _gelu(zuv[:, :D_MODEL])
    vn = _layer_norm(_gelu(zuv[:, D_MODEL:]), lng_ref[...], lnb_ref[...])
    if emit_vnorm:
        vn_ref[...] = vn
    vnb = vn.astype(BF16)

    row = lax.broadcasted_iota(jnp.int32, (CHUNK, CHUNK), 0)
    col = lax.broadcasted_iota(jnp.int32, (CHUNK, CHUNK), 1)
    same_chunk = (row & ~(chunk_len - 1)) == (col & ~(chunk_len - 1))
    mix_mask = (col <= row) & same_chunk
    for g in range(A_GROUPS):
        gs = slice(g * A_GROUP_DIM, (g + 1) * A_GROUP_DIM)
        wm = jnp.where(mix_mask, wmix_ref[g], jnp.zeros((), BF16))
        for c in range(tm // CHUNK):
            rs = slice(c * CHUNK, (c + 1) * CHUNK)
            s = _dot(wm, vnb[rs, gs]) + bmix_ref[:, gs]
            aout_ref[rs, gs] = (u[rs, gs] * s).astype(BF16)

    pa = _dot(aout_ref[...], wpa_ref[...])
    gates = _dot(xb, wg_ref[...])
    a_ref[...] = (_sigmoid(gates[:, :D_MODEL]) * pa).astype(BF16)
    sgb_ref[...] = _sigmoid(gates[:, D_MODEL:]).astype(BF16)

    q_ref[...] = (_dot(xb, wq_ref[...]) * q_scale).astype(BF16)
    kv = _dot(xb, wkv_ref[...])
    for h in range(N_KV_HEADS):
        k_ref[pl.ds(h, tm, stride=N_KV_HEADS), :] = kv[:, h * HEAD_DIM:(h + 1) * HEAD_DIM]
        v_ref[pl.ds(h, tm, stride=N_KV_HEADS), :] = kv[:, KV_DIM + h * HEAD_DIM:KV_DIM + (h + 1) * HEAD_DIM]
    kb_ref[...] = kv[:, :KV_DIM].astype(BF16)
    if transpose_v:
        vb_ref[0] = kv[:, KV_DIM:].T.astype(BF16)
    else:
        vb_ref[...] = kv[:, KV_DIM:].astype(BF16)
    qi_ref[...] = _dot(xb, wqi_ref[...]).astype(BF16)
    kw_ref[...] = _dot(xb, wkw_ref[...])


def _input_proj(x, wts, *, chunk_len, emit_vnorm, seq_len=None):
    t = x.shape[0]
    tm = min(TOKEN_TILE, t)
    assert t % tm == 0 and tm % CHUNK == 0
    row_spec = lambda n, rows_per_token=1: pl.BlockSpec((tm * rows_per_token, n), lambda i: (i, 0))
    out_shapes = [
        jax.ShapeDtypeStruct((t, D_MODEL), BF16),
        jax.ShapeDtypeStruct((t, D_MODEL), BF16),
        jax.ShapeDtypeStruct((t, D_MODEL), BF16),
        jax.ShapeDtypeStruct((t * N_KV_HEADS, HEAD_DIM), F32),
        jax.ShapeDtypeStruct((t * N_KV_HEADS, HEAD_DIM), F32),
        jax.ShapeDtypeStruct((t, KV_DIM), BF16),
        jax.ShapeDtypeStruct((t, KV_DIM), BF16),
        jax.ShapeDtypeStruct((t, N_IDX_HEADS * IDX_DIM), BF16),
        jax.ShapeDtypeStruct((t, LANES), F32),
    ]
    out_specs = [row_spec(s.shape[1], s.shape[0] // t) for s in out_shapes]
    if seq_len is not None:
        assert seq_len % tm == 0 and t % seq_len == 0
        tiles = seq_len // tm
        out_shapes[6] = jax.ShapeDtypeStruct((t // seq_len, KV_DIM, seq_len), BF16)
        out_specs[6] = pl.BlockSpec((1, KV_DIM, tm), lambda i: (i // tiles, 0, i % tiles))
    if emit_vnorm:
        out_shapes.append(jax.ShapeDtypeStruct((t, D_MODEL), F32))
        out_specs.append(row_spec(D_MODEL))
    weights = (wts["wuv"], wts["wq"], wts["wkv"], wts["wqi"], wts["wkw"], wts["wg"], wts["ln_v_g"],
               wts["ln_v_b"], wts["wmix"], wts["bmix"], wts["wpa"])
    return pl.pallas_call(
        functools.partial(_input_proj_kernel, chunk_len=chunk_len, emit_vnorm=emit_vnorm,
                          transpose_v=seq_len is not None,
                          q_scale=HEAD_DIM ** -0.5 * (LOG2_E if seq_len is not None else 1.0)),
        grid=(t // tm,),
        in_specs=[row_spec(D_MODEL)] + [_resident(w.shape) for w in weights],
        out_specs=out_specs,
        out_shape=out_shapes,
        scratch_shapes=[pltpu.VMEM((tm, D_MODEL), BF16)],
        compiler_params=pltpu.CompilerParams(dimension_semantics=("arbitrary",),
                                             vmem_limit_bytes=VMEM_LIMIT_BYTES),
        name="input_proj_sample" if emit_vnorm else "input_proj_prompt",
    )(x, *weights)


def _merge_ffn_kernel(x_ref, a_ref, sgb_ref, bo_ref, wpb_ref, wo_ref, ln1g_ref, ln1b_ref, wgate_ref,
                      wup_ref, wdown_ref, ln2g_ref, ln2b_ref, y_ref, *, alpha):
    pb = _dot(bo_ref[...], wpb_ref[...])
    merged = a_ref[...].astype(F32) + sgb_ref[...].astype(F32) * pb
    mix = _dot(merged.astype(BF16), wo_ref[...])
    x1 = _layer_norm(alpha * x_ref[...] + mix, ln1g_ref[...], ln1b_ref[...])
    x1b = x1.astype(BF16)
    hg = _dot(x1b, wgate_ref[...])
    hu = _dot(x1b, wup_ref[...])
    h = (hg * _sigmoid(hg)) * hu
    f = _dot(h.astype(BF16), wdown_ref[...])
    y_ref[...] = _layer_norm(alpha * x1 + f, ln2g_ref[...], ln2b_ref[...])


def _merge_ffn(x, a, sgb, bo, wts, *, alpha, name):
    t = x.shape[0]
    tm = min(TOKEN_TILE, t)
    assert t % tm == 0
    row_spec = pl.BlockSpec((tm, D_MODEL), lambda i: (i, 0))
    weights = (wts["wpb"], wts["wo"], wts["ln1_g"], wts["ln1_b"], wts["wgate"], wts["wup"], wts["wdown"],
               wts["ln2_g"], wts["ln2_b"])
    return pl.pallas_call(
        functools.partial(_merge_ffn_kernel, alpha=alpha),
        grid=(t // tm,),
        in_specs=[row_spec] * 4 + [_resident(w.shape) for w in weights],
        out_specs=row_spec,
        out_shape=jax.ShapeDtypeStruct((t, D_MODEL), F32),
        compiler_params=pltpu.CompilerParams(dimension_semantics=("arbitrary",),
                                             vmem_limit_bytes=VMEM_LIMIT_BYTES),
        name=name,
    )(x, a, sgb, bo, *weights)


def _lane_partial(x):
    acc = x[:, :LANES]
    for i in range(1, x.shape[1] // LANES):
        acc = acc + x[:, i * LANES:(i + 1) * LANES]
    return acc


def _kth_largest_key(count_ge, shape, topk):
    def bit_body(i, carry):
        u, cnt_u = carry
        cand = u | lax.shift_left(jnp.int32(1), 31 - i)
        cnt = count_ge(_key_to_float(cand))
        take = cnt >= topk
        return jnp.where(take, cand, u), jnp.where(take, cnt, cnt_u)
    return lax.fori_loop(0, 32, bit_body, (jnp.zeros(shape, jnp.int32), jnp.zeros(shape, F32)))


def _sublane_partial(x):
    parts = [x[i * SUBLANES:(i + 1) * SUBLANES] for i in range(x.shape[0] // SUBLANES)]
    while len(parts) > 1:
        parts = [a + b for a, b in zip(parts[::2], parts[1::2])] + parts[len(parts) - len(parts) % 2:]
    return parts[0]


def _select_chunk_t(s, thr, need, carry, tril):
    gt = s > thr
    eq = s == thr
    eqf = eq.astype(F32)
    inc = _dot(tril, eqf.astype(BF16))
    before = carry + inc - eqf
    sel = gt | (eq & (before < need))
    return jnp.where(sel, 0.0, NEG_INF), carry + inc[inc.shape[0] - 1:]


def _threshold_bias(s, thr):
    return jnp.where(s >= thr, 0.0, NEG_INF)


def _select_chunk(s, thr, need, carry, tri):
    gt = s > thr
    eq = s == thr
    eqf = eq.astype(F32)
    inc = _dot(eqf.astype(BF16), tri)
    before = carry + inc - eqf
    sel = gt | (eq & (before < need))
    bias = jnp.where(sel, 0.0, NEG_INF)
    return bias, carry + inc[:, inc.shape[1] - 1:]


def _prompt_attn_kernel(q_ref, qi_ref, kwq_ref, kb_ref, vt_ref, kw_ref, tril_ref, o_ref,
                        sc_ref, m_ref, acc_ref, *, topk):
    tq = q_ref.shape[1]
    ck = KEY_CHUNK
    j = pl.program_id(1)
    n_chunks = lax.shift_right_logical((j + 1) * tq + (ck - 1), int(math.log2(ck)))
    t = j * tq + lax.broadcasted_iota(jnp.int32, (1, tq), 1)
    chunk = lambda c: pl.ds(pl.multiple_of(c * ck, ck), ck)

    qi = qi_ref[0]
    w_t = kwq_ref[0].T[IDX_DIM:IDX_DIM + N_IDX_HEADS]
    w_t = (w_t * (N_IDX_HEADS ** -0.5)) * (IDX_DIM ** -0.5)
    qi_pairs = [jnp.concatenate([qi[:, h * IDX_DIM:(h + 1) * IDX_DIM] for h in (2 * p, 2 * p + 1)], axis=0)
                for p in range(N_IDX_HEADS // 2)]
    w_pairs = [jnp.concatenate([w_t[h:h + 1] for h in (2 * p, 2 * p + 1)], axis=1)
               for p in range(N_IDX_HEADS // 2)]

    def score_chunk(c, carry):
        kc = kw_ref[0, chunk(c), :][:, :IDX_DIM].astype(BF16)
        acc = jnp.zeros((ck, tq), F32)
        for p in range(N_IDX_HEADS // 2):
            d = jnp.maximum(_dot_nt(kc, qi_pairs[p]), 0.0) * w_pairs[p]
            acc = acc + (d[:, :tq] + d[:, tq:])
        kpos = c * ck + lax.broadcasted_iota(jnp.int32, (ck, 1), 0)
        sc_ref[chunk(c), :] = jnp.where(kpos <= t, acc, NEG_INF)
        return carry

    lax.fori_loop(0, n_chunks, score_chunk, 0)

    def count(pred):
        n_acc = 4

        def body(c, accs):
            accs = list(accs)
            for i in range(ck // SUBLANES):
                rows = pl.ds(pl.multiple_of(c * ck + i * SUBLANES, SUBLANES), SUBLANES)
                accs[i % n_acc] = accs[i % n_acc] + pred(sc_ref[rows, :]).astype(F32)
            return tuple(accs)

        zero = jnp.zeros((SUBLANES, tq), F32)
        accs = lax.fori_loop(0, n_chunks, body, (zero,) * n_acc)
        return jnp.sum((accs[0] + accs[1]) + (accs[2] + accs[3]), axis=0, keepdims=True)

    u, cnt_u = _kth_largest_key(lambda cand: count(lambda s: s >= cand), (1, tq), topk)
    small = t < topk
    thr = jnp.where(small, F32_LOWEST, _key_to_float(u))
    has_ties = jnp.max(jnp.where(small, 0.0, cnt_u - topk)) > 0.0

    @pl.when(has_ties)
    def _():
        need = topk - count(lambda s: s > thr)

        def select_chunk(c, carry):
            bias, carry = _select_chunk_t(sc_ref[chunk(c), :], thr, need, carry, tril_ref[...])
            sc_ref[chunk(c), :] = bias
            return carry

        lax.fori_loop(0, n_chunks, select_chunk, jnp.zeros((1, tq), F32))

    @pl.when(jnp.logical_not(has_ties))
    def _():
        def select_chunk(c, carry):
            sc_ref[chunk(c), :] = _threshold_bias(sc_ref[chunk(c), :], thr)
            return carry

        lax.fori_loop(0, n_chunks, select_chunk, 0)

    q = q_ref[0]
    ones = jnp.ones((BF16_SUBLANES, ck), BF16)
    q2s = [jnp.concatenate([q[:, (g * Q_PER_KV + i) * HEAD_DIM:(g * Q_PER_KV + i + 1) * HEAD_DIM]
                            for i in range(Q_PER_KV)], axis=0) for g in range(N_KV_HEADS)]
    m_ref[...] = jnp.full(m_ref.shape, M_INIT, F32)
    acc_ref[...] = jnp.zeros(acc_ref.shape, F32)

    def attend_chunk(c, carry):
        bias = jnp.concatenate([sc_ref[chunk(c), :]] * Q_PER_KV, axis=1)
        for g in range(N_KV_HEADS):
            gs = slice(g * HEAD_DIM, (g + 1) * HEAD_DIM)
            s = _dot_nt(kb_ref[0, chunk(c), gs], q2s[g]) + bias
            m_old = m_ref[g][:1]
            m_new = jnp.maximum(m_old, jnp.max(s, axis=0, keepdims=True))
            p = jnp.exp2(s - m_new).astype(BF16)
            v1 = jnp.concatenate([vt_ref[0, gs, chunk(c)], ones], axis=0)
            acc_ref[g] = jnp.exp2(m_old - m_new) * acc_ref[g] + _dot(v1, p)
            m_ref[g] = jnp.broadcast_to(m_new, m_ref.shape[1:])
        return carry

    lax.fori_loop(0, n_chunks, attend_chunk, 0)
    for g in range(N_KV_HEADS):
        acc = acc_ref[g]
        o_t = acc[:HEAD_DIM] / acc[HEAD_DIM:HEAD_DIM + 1]
        for i in range(Q_PER_KV):
            h = g * Q_PER_KV + i
            o_ref[0, :, h * HEAD_DIM:(h + 1) * HEAD_DIM] = o_t[:, i * tq:(i + 1) * tq].T.astype(BF16)


def _prompt_attn(q, qi, kw, kb, vt, tril):
    b, s, _ = q.shape
    tq = Q_TILE
    assert s % KEY_CHUNK == 0 and s % tq == 0 and tq % LANES == 0
    topk = min(TOPK_MAX, s // 4)
    blk = lambda n: pl.BlockSpec((1, tq, n), lambda i, j: (i, j, 0))
    seq = lambda n: pl.BlockSpec((1, s, n), lambda i, j: (i, 0, 0))
    return pl.pallas_call(
        functools.partial(_prompt_attn_kernel, topk=topk),
        grid=(b, s // tq),
        in_specs=[blk(D_MODEL), blk(N_IDX_HEADS * IDX_DIM), blk(LANES), seq(KV_DIM),
                  pl.BlockSpec((1, KV_DIM, s), lambda i, j: (i, 0, 0)), seq(LANES), _resident(tril.shape)],
        out_specs=blk(D_MODEL),
        out_shape=jax.ShapeDtypeStruct((b, s, D_MODEL), BF16),
        scratch_shapes=[pltpu.VMEM((s, tq), F32),
                        pltpu.VMEM((N_KV_HEADS, SUBLANES, Q_PER_KV * tq), F32),
                        pltpu.VMEM((N_KV_HEADS, HEAD_DIM + BF16_SUBLANES, Q_PER_KV * tq), F32)],
        compiler_params=pltpu.CompilerParams(dimension_semantics=("arbitrary", "arbitrary"),
                                             vmem_limit_bytes=VMEM_LIMIT_BYTES),
        name="prompt_attn",
    )(q, qi, kw, kb, vt, kw, tril)


def _sample_score_kernel(pt_ref, qi_ref, w_ref, knew_ref, *refs, n_pages, t_new):
    del pt_ref
    page_refs = refs[:n_pages]
    sp_ref, sn_ref = refs[n_pages:]
    qi = qi_ref[0]
    w = (w_ref[0] * (N_IDX_HEADS ** -0.5)) * (IDX_DIM ** -0.5)

    def head_sum(dots):
        r = jnp.maximum(dots, 0.0) * w
        return [jnp.sum(r[i * N_IDX_HEADS:(i + 1) * N_IDX_HEADS], axis=0, keepdims=True)
                for i in range(t_new)]

    group = 4
    for p0 in range(0, n_pages, group):
        keys_t = jnp.concatenate([page_refs[p][0] for p in range(p0, p0 + group)], axis=1).astype(BF16)
        for i, r in enumerate(head_sum(_dot(qi, keys_t))):
            sp_ref[0, i:i + 1, p0 * PAGE_SIZE:(p0 + group) * PAGE_SIZE] = r

    knew = jnp.concatenate([knew_ref[0][:, :IDX_DIM],
                            jnp.zeros((LANES - knew_ref.shape[1], IDX_DIM), F32)], axis=0).astype(BF16)
    kpos = lax.broadcasted_iota(jnp.int32, (1, LANES), 1)
    for i, r in enumerate(head_sum(_dot_nt(qi, knew))):
        sn_ref[0, i:i + 1, :] = jnp.where(kpos <= i, r, NEG_INF)


def _sample_scores(page_table, qi32, w32, knew, cache_kidx_t):
    db, n_pages = page_table.shape
    t_new = qi32.shape[1] // N_IDX_HEADS
    assert n_pages % 4 == 0
    page_spec = lambda p: pl.BlockSpec((1, IDX_DIM, PAGE_SIZE), lambda b, pt: (pt[b, p], 0, 0))
    grid_spec = pltpu.PrefetchScalarGridSpec(
        num_scalar_prefetch=1,
        grid=(db,),
        in_specs=[pl.BlockSpec((1,) + qi32.shape[1:], lambda b, pt: (b, 0, 0)),
                  pl.BlockSpec((1,) + w32.shape[1:], lambda b, pt: (b, 0, 0)),
                  pl.BlockSpec((1,) + knew.shape[1:], lambda b, pt: (b, 0, 0))]
                 + [page_spec(p) for p in range(n_pages)],
        out_specs=[pl.BlockSpec((1, t_new, n_pages * PAGE_SIZE), lambda b, pt: (b, 0, 0)),
                   pl.BlockSpec((1, t_new, LANES), lambda b, pt: (b, 0, 0))],
    )
    return pl.pallas_call(
        functools.partial(_sample_score_kernel, n_pages=n_pages, t_new=t_new),
        grid_spec=grid_spec,
        out_shape=[jax.ShapeDtypeStruct((db, t_new, n_pages * PAGE_SIZE), F32),
                   jax.ShapeDtypeStruct((db, t_new, LANES), F32)],
        compiler_params=pltpu.CompilerParams(dimension_semantics=("arbitrary",),
                                             vmem_limit_bytes=VMEM_LIMIT_BYTES),
        name="sample_scores",
    )(page_table, qi32, w32, knew, *([cache_kidx_t] * n_pages))


def _sample_select_kernel(sp_ref, sn_ref, tri_ref, bp_ref, bn_ref, *, topk):
    rows, past = sp_ref.shape
    ck = KEY_CHUNK
    n_chunks = past // ck

    def count(pred):
        def body(c, cnt):
            s = sp_ref[:, pl.ds(pl.multiple_of(c * ck, ck), ck)]
            return cnt + _lane_partial(pred(s).astype(F32))
        part = lax.fori_loop(0, n_chunks, body, pred(sn_ref[...]).astype(F32))
        return jnp.sum(part, axis=-1, keepdims=True)

    u, cnt_u = _kth_largest_key(lambda cand: count(lambda s: s >= cand), (rows, 1), topk)
    thr = _key_to_float(u)
    has_ties = jnp.max(cnt_u) > topk

    @pl.when(has_ties)
    def _():
        need = topk - count(lambda s: s > thr)

        def select_chunk(c, carry):
            cs = pl.ds(pl.multiple_of(c * ck, ck), ck)
            bias, carry = _select_chunk(sp_ref[:, cs], thr, need, carry, tri_ref[...])
            bp_ref[:, cs] = bias
            return carry

        carry = lax.fori_loop(0, n_chunks, select_chunk, jnp.zeros((rows, 1), F32))
        bias, _ = _select_chunk(sn_ref[...], thr, need, carry, tri_ref[:LANES, :LANES])
        bn_ref[...] = bias

    @pl.when(jnp.logical_not(has_ties))
    def _():
        def select_chunk(c, carry):
            cs = pl.ds(pl.multiple_of(c * ck, ck), ck)
            bp_ref[:, cs] = _threshold_bias(sp_ref[:, cs], thr)
            return carry

        lax.fori_loop(0, n_chunks, select_chunk, 0)
        bn_ref[...] = _threshold_bias(sn_ref[...], thr)


def _sample_select(sp, sn, tri, *, topk):
    n, past = sp.shape
    rows = min(SAMPLE_ROWS, n)
    assert n % rows == 0 and past % KEY_CHUNK == 0
    return pl.pallas_call(
        functools.partial(_sample_select_kernel, topk=topk),
        grid=(n // rows,),
        in_specs=[pl.BlockSpec((rows, past), lambda i: (i, 0)),
                  pl.BlockSpec((rows, LANES), lambda i: (i, 0)),
                  _resident(tri.shape)],
        out_specs=[pl.BlockSpec((rows, past), lambda i: (i, 0)),
                   pl.BlockSpec((rows, LANES), lambda i: (i, 0))],
        out_shape=[jax.ShapeDtypeStruct((n, past), F32), jax.ShapeDtypeStruct((n, LANES), F32)],
        compiler_params=pltpu.CompilerParams(dimension_semantics=("arbitrary",),
                                             vmem_limit_bytes=VMEM_LIMIT_BYTES),
        name="sample_select",
    )(sp, sn, tri)


def _sample_attn_kernel(pt_ref, q_ref, bp_ref, bn_ref, knew_ref, vnew_ref, *refs, n_pages, t_new):
    del pt_ref
    k_refs = refs[:n_pages]
    v_refs = refs[n_pages:2 * n_pages]
    o_ref, m_ref, l_ref, acc_ref = refs[2 * n_pages:]
    g = pl.program_id(1)
    q = q_ref[0]

    @pl.when(g == 0)
    def _():
        m_ref[...] = jnp.full(m_ref.shape, M_INIT, F32)
        l_ref[...] = jnp.zeros(l_ref.shape, F32)
        acc_ref[...] = jnp.zeros(acc_ref.shape, F32)

    def row_bias(b):
        return jnp.concatenate([jnp.broadcast_to(b[i:i + 1], (N_HEADS, b.shape[1])) for i in range(t_new)],
                               axis=0)

    heads = [slice(h * HEAD_DIM, (h + 1) * HEAD_DIM) for h in range(N_KV_HEADS)]

    def update(keys, vals, bias):
        s = bias
        for h in range(N_KV_HEADS):
            s = s + _dot_nt(q[:, heads[h]], keys[h])
        m_old = m_ref[...][:, :1]
        m_new = jnp.maximum(m_old, jnp.max(s, axis=-1, keepdims=True))
        p = jnp.exp(s - m_new)
        alpha = jnp.exp(m_old - m_new)
        pb = p.astype(BF16)
        l_ref[...] = alpha * l_ref[...] + jnp.sum(p, axis=-1, keepdims=True)
        acc_ref[...] = alpha * acc_ref[...] + jnp.concatenate([_dot(pb, vals[h]) for h in range(N_KV_HEADS)],
                                                              axis=1)
        m_ref[...] = jnp.broadcast_to(m_new, m_ref.shape)

    def page_head(page_refs, h):
        return jnp.concatenate([r[0, pl.ds(h, PAGE_SIZE, stride=N_KV_HEADS), :] for r in page_refs],
                               axis=0).astype(BF16)

    update([page_head(k_refs, h) for h in range(N_KV_HEADS)],
           [page_head(v_refs, h) for h in range(N_KV_HEADS)], row_bias(bp_ref[0]))

    @pl.when(g == pl.num_programs(1) - 1)
    def _():
        update([knew_ref[0][:, hs] for hs in heads], [vnew_ref[0][:, hs] for hs in heads], row_bias(bn_ref[0]))
        o = acc_ref[...] / l_ref[...][:, :1]
        kv_head = lax.shift_right_logical(
            lax.broadcasted_iota(jnp.int32, (o.shape[0], 1), 0) & (N_HEADS - 1), int(math.log2(Q_PER_KV)))
        out = jnp.zeros((o.shape[0], HEAD_DIM), F32)
        for k in range(N_KV_HEADS):
            out = out + jnp.where(kv_head == k, o[:, k * HEAD_DIM:(k + 1) * HEAD_DIM], 0.0)
        o_ref[0] = out.astype(BF16)


def _sample_attn(page_table, q32, bias_past, bias_new, knew, vnew, cache_k, cache_v):
    db, n_pages = page_table.shape
    rows = q32.shape[1]
    t_new = rows // N_HEADS
    pages = min(SAMPLE_PAGES, n_pages)
    assert n_pages % pages == 0
    k_spec = lambda p: pl.BlockSpec((1, PAGE_SIZE * N_KV_HEADS, HEAD_DIM),
                                    lambda b, g, pt: (pt[b, g * pages + p], 0, 0))
    per_seq = lambda shape: pl.BlockSpec((1,) + shape[1:], lambda b, g, pt: (b, 0, 0))
    grid_spec = pltpu.PrefetchScalarGridSpec(
        num_scalar_prefetch=1,
        grid=(db, n_pages // pages),
        in_specs=[per_seq(q32.shape),
                  pl.BlockSpec((1, t_new, pages * PAGE_SIZE), lambda b, g, pt: (b, 0, g)),
                  per_seq(bias_new.shape), per_seq(knew.shape), per_seq(vnew.shape)]
                 + [k_spec(p) for p in range(pages)] * 2,
        out_specs=pl.BlockSpec((1, rows, HEAD_DIM), lambda b, g, pt: (b, 0, 0)),
        scratch_shapes=[pltpu.VMEM((rows, LANES), F32), pltpu.VMEM((rows, LANES), F32),
                        pltpu.VMEM((rows, KV_DIM), F32)],
    )
    return pl.pallas_call(
        functools.partial(_sample_attn_kernel, n_pages=pages, t_new=t_new),
        grid_spec=grid_spec,
        out_shape=jax.ShapeDtypeStruct((db, rows, HEAD_DIM), BF16),
        compiler_params=pltpu.CompilerParams(dimension_semantics=("arbitrary", "arbitrary"),
                                             vmem_limit_bytes=VMEM_LIMIT_BYTES),
        name="sample_attn",
    )(page_table, q32, bias_past, bias_new, knew, vnew, *([cache_k] * pages), *([cache_v] * pages))


def _sample_mixer_b(page_table, q, qi, kw, kb, vb, cache_k, cache_v, cache_kidx, tri):
    db, n_pages = page_table.shape
    n_pool = cache_k.shape[0]
    t_new = q.shape[0] // db
    past = n_pages * PAGE_SIZE
    rows = t_new * N_IDX_HEADS
    qi32 = qi.reshape(db, rows, IDX_DIM)
    w32 = kw[:, IDX_DIM:IDX_DIM + N_IDX_HEADS].reshape(db, rows, 1)
    knew_idx = jnp.pad(kw.reshape(db, t_new, LANES), ((0, 0), (0, SUBLANES - t_new), (0, 0)))
    sp, sn = _sample_scores(page_table, qi32, w32, knew_idx, jnp.swapaxes(cache_kidx, 1, 2))
    bias_p, bias_n = _sample_select(sp.reshape(db * t_new, past), sn.reshape(db * t_new, LANES), tri,
                                    topk=min(TOPK_MAX, (past + t_new) // 4))
    q5 = q.reshape(db, t_new, N_KV_HEADS, Q_PER_KV, 1, HEAD_DIM)
    onehot = jnp.eye(N_KV_HEADS, dtype=BF16).reshape(1, 1, N_KV_HEADS, 1, N_KV_HEADS, 1)
    q32 = (q5 * onehot).reshape(db, t_new * N_HEADS, KV_DIM)
    pad_new = lambda arr: jnp.pad(arr.reshape(db, t_new, KV_DIM), ((0, 0), (0, NEW_ROWS - t_new), (0, 0)))
    bo = _sample_attn(page_table, q32, bias_p.reshape(db, t_new, past), bias_n.reshape(db, t_new, LANES),
                      pad_new(kb), pad_new(vb), cache_k.reshape(n_pool, PAGE_SIZE * N_KV_HEADS, HEAD_DIM),
                      cache_v.reshape(n_pool, PAGE_SIZE * N_KV_HEADS, HEAD_DIM))
    return bo.reshape(db * t_new, D_MODEL)


def _prepare_weights(layer, w_in, ln_v_g, ln_v_b, w_s, b_s, w_pa, w_pb, w_o, ln1_g, ln1_b, w_gate, w_up,
                     w_down, ln2_g, ln2_b, t_new):
    w = w_in[layer]
    o_q = 2 * D_MODEL
    o_kv = o_q + D_MODEL
    o_qi = o_kv + 2 * KV_DIM
    o_ki = o_qi + N_IDX_HEADS * IDX_DIM
    o_g = o_ki + IDX_DIM + N_IDX_HEADS
    row = lambda v: v[layer].reshape(1, -1)
    reps = CHUNK // t_new
    shared = dict(
        wuv=w[:, :o_q].astype(BF16), wq=w[:, o_q:o_kv].astype(BF16), wkv=w[:, o_kv:o_qi].astype(BF16),
        wqi=w[:, o_qi:o_ki].astype(BF16),
        wkw=jnp.pad(w[:, o_ki:o_g], ((0, 0), (0, LANES - (o_g - o_ki)))).astype(BF16),
        wg=w[:, o_g:].astype(BF16),
        ln_v_g=row(ln_v_g), ln_v_b=row(ln_v_b), wpa=w_pa[layer].astype(BF16),
        wpb=w_pb[layer].astype(BF16), wo=w_o[layer].astype(BF16), ln1_g=row(ln1_g), ln1_b=row(ln1_b),
        wgate=w_gate[layer].astype(BF16), wup=w_up[layer].astype(BF16), wdown=w_down[layer].astype(BF16),
        ln2_g=row(ln2_g), ln2_b=row(ln2_b))
    prompt = dict(shared, wmix=w_s[layer].astype(BF16),
                  bmix=jnp.repeat(b_s[layer].T, A_GROUP_DIM, axis=1))
    sample = dict(shared, wmix=jnp.tile(w_s[layer][:, :t_new, :t_new], (1, reps, reps)).astype(BF16),
                  bmix=jnp.repeat(jnp.tile(b_s[layer][:, :t_new].T, (reps, 1)), A_GROUP_DIM, axis=1))
    return prompt, sample


def kernel(x_prompt, x_sample, cache_k, cache_v, cache_kidx, page_table, w_in, ln_v_g, ln_v_b, w_s, b_s,
           w_pa, w_pb, w_o, ln1_g, ln1_b, w_gate, w_up, w_down, ln2_g, ln2_b):
    depth = w_in.shape[0]
    alpha = (2 * depth) ** 0.25
    b, s, _ = x_prompt.shape
    db, t_new, _ = x_sample.shape
    n_pages = page_table.shape[1]
    n_pool = cache_k.shape[1]
    past = n_pages * PAGE_SIZE
    assert t_new <= CHUNK and CHUNK % t_new == 0 and t_new <= NEW_ROWS
    tri = jnp.triu(jnp.ones((KEY_CHUNK, KEY_CHUNK), BF16))
    tril = jnp.tril(jnp.ones((KEY_CHUNK, KEY_CHUNK), BF16))

    y_p = x_prompt.reshape(b * s, D_MODEL)
    y_s = x_sample.reshape(db * t_new, D_MODEL)
    outs = [[] for _ in range(7)]
    for layer in range(depth):
        wp, ws = _prepare_weights(layer, w_in, ln_v_g, ln_v_b, w_s, b_s, w_pa, w_pb, w_o, ln1_g, ln1_b,
                                  w_gate, w_up, w_down, ln2_g, ln2_b, t_new)

        a, sgb, q, k_p, v_p, kb, vt, qi, kw = _input_proj(y_p, wp, chunk_len=CHUNK, emit_vnorm=False, seq_len=s)
        seq = lambda arr: arr.reshape(b, s, arr.shape[-1])
        bo = _prompt_attn(seq(q), seq(qi), seq(kw), seq(kb), vt, tril)
        y_p = _merge_ffn(y_p, a, sgb, bo.reshape(b * s, D_MODEL), wp, alpha=alpha, name="merge_ffn_prompt")
        outs[0].append(k_p.reshape(b, s, N_KV_HEADS, HEAD_DIM))
        outs[1].append(v_p.reshape(b, s, N_KV_HEADS, HEAD_DIM))
        outs[2].append(kw[:, :IDX_DIM].reshape(b, s, IDX_DIM))

        a, sgb, q, k_s, v_s, kb, vb, qi, kw, vn = _input_proj(y_s, ws, chunk_len=t_new, emit_vnorm=True)
        bo = _sample_mixer_b(page_table, q, qi, kw, kb, vb, cache_k[layer], cache_v[layer], cache_kidx[layer],
                             tri)
        y_s = _merge_ffn(y_s, a, sgb, bo, ws, alpha=alpha, name="merge_ffn_sample")
        outs[3].append(k_s.reshape(db, t_new, N_KV_HEADS, HEAD_DIM))
        outs[4].append(v_s.reshape(db, t_new, N_KV_HEADS, HEAD_DIM))
        outs[5].append(kw[:, :IDX_DIM].reshape(db, t_new, IDX_DIM))
        outs[6].append(vn.reshape(db, t_new, D_MODEL))

    return (y_p.reshape(b, s, D_MODEL), y_s.reshape(db, t_new, D_MODEL), *[jnp.stack(o) for o in outs])
```

```python
import functools
import math

import jax
import jax.numpy as jnp
from jax import lax
from jax.experimental import pallas as pl
from jax.experimental.pallas import tpu as pltpu
from jax.experimental.pallas import tpu_sc as plsc

D_MODEL = 1024
CHUNK = 128
A_GROUPS = 8
A_GROUP_DIM = D_MODEL // A_GROUPS
N_HEADS = 8
N_KV_HEADS = 4
Q_PER_KV = N_HEADS // N_KV_HEADS
HEAD_DIM = D_MODEL // N_HEADS
KV_DIM = N_KV_HEADS * HEAD_DIM
N_IDX_HEADS = 8
IDX_DIM = 64
TOPK_MAX = 256
PAGE_SIZE = 128
LN_EPS = 1e-5

LANES = 128
SUBLANES = 8
BF16_SUBLANES = 16
VMEM_LIMIT_BYTES = 56 * 1024 * 1024

TOKEN_TILE = 512
Q_TILE = 512
KEY_CHUNK = 512
SAMPLE_ROWS = 64
SC_CORES = 2
SC_SUBCORES = 16
SC_LANES = 16
SC_GATHER_WINDOW = 64
NEW_ROWS = 128

F32 = jnp.float32
BF16 = jnp.bfloat16
NEG_INF = float("-inf")
F32_LOWEST = float(jnp.finfo(jnp.float32).min)
M_INIT = -1e30
LOG2_E = math.log2(math.e)
INT32_MIN = -(2 ** 31)


def _dot(a, b):
    return jnp.dot(a, b, preferred_element_type=F32)


def _dot_nt(a, b):
    return lax.dot_general(a, b, (((1,), (1,)), ((), ())), preferred_element_type=F32)


def _gelu(x):
    c = math.sqrt(2.0 / math.pi)
    return x * (0.5 * (1.0 + jnp.tanh(c * (x + 0.044715 * (x * x * x)))))


def _sigmoid(x):
    return 1.0 / (1.0 + jnp.exp(-x))


def _layer_norm(x, g, b):
    mu = jnp.mean(x, axis=-1, keepdims=True)
    xc = x - mu
    var = jnp.mean(xc * xc, axis=-1, keepdims=True)
    return xc * lax.rsqrt(var + LN_EPS) * g + b


def _key_to_float(u):
    c = u ^ jnp.int32(INT32_MIN)
    bits = c ^ ((c >> 31) & jnp.int32(0x7FFFFFFF))
    return lax.bitcast_convert_type(bits, F32)


def _resident(shape):
    nd = len(shape)
    return pl.BlockSpec(shape, lambda *_: (0,) * nd, pipeline_mode=pl.Buffered(1))


def _input_proj_kernel(x_ref, wuv_ref, wq_ref, wkv_ref, wqi_ref, wkw_ref, wg_ref, lng_ref, lnb_ref,
                       wmix_ref, bmix_ref, wpa_ref,
                       a_ref, sgb_ref, q_ref, k_ref, v_ref, kb_ref, vb_ref, qi_ref, kw_ref, *rest,
                       chunk_len, emit_vnorm, transpose_v, q_scale):
    if emit_vnorm:
        vn_ref, aout_ref = rest
    else:
        (aout_ref,) = rest
    tm = x_ref.shape[0]
    xb = x_ref[...].astype(BF16)

    zuv = _dot(xb, wuv_ref[...])
    u = _gelu(zuv[:, :D_MODEL])
    vn = _layer_norm(_gelu(zuv[:, D_MODEL:]), lng_ref[...], lnb_ref[...])
    if emit_vnorm:
        vn_ref[...] = vn
    vnb = vn.astype(BF16)

    row = lax.broadcasted_iota(jnp.int32, (CHUNK, CHUNK), 0)
    col = lax.broadcasted_iota(jnp.int32, (CHUNK, CHUNK), 1)
    same_chunk = (row & ~(chunk_len - 1)) == (col & ~(chunk_len - 1))
    mix_mask = (col <= row) & same_chunk
    for g in range(A_GROUPS):
        gs = slice(g * A_GROUP_DIM, (g + 1) * A_GROUP_DIM)
        wm = jnp.where(mix_mask, wmix_ref[g], jnp.zeros((), BF16))
        for c in range(tm // CHUNK):
            rs = slice(c * CHUNK, (c + 1) * CHUNK)
            s = _dot(wm, vnb[rs, gs]) + bmix_ref[:, gs]
            aout_ref[rs, gs] = (u[rs, gs] * s).astype(BF16)

    pa = _dot(aout_ref[...], wpa_ref[...])
    gates = _dot(xb, wg_ref[...])
    a_ref[...] = (_sigmoid(gates[:, :D_MODEL]) * pa).astype(BF16)
    sgb_ref[...] = _sigmoid(gates[:, D_MODEL:]).astype(BF16)

    q_ref[...] = (_dot(xb, wq_ref[...]) * q_scale).astype(BF16)
    kv = _dot(xb, wkv_ref[...])
    for h in range(N_KV_HEADS):
        k_ref[pl.ds(h, tm, stride=N_KV_HEADS), :] = kv[:, h * HEAD_DIM:(h + 1) * HEAD_DIM]
        v_ref[pl.ds(h, tm, stride=N_KV_HEADS), :] = kv[:, KV_DIM + h * HEAD_DIM:KV_DIM + (h + 1) * HEAD_DIM]
    kb_ref[...] = kv[:, :KV_DIM].astype(BF16)
    if transpose_v:
        vb_ref[0] = kv[:, KV_DIM:].T.astype(BF16)
    else:
        vb_ref[...] = kv[:, KV_DIM:].astype(BF16)
    qi_ref[...] = _dot(xb, wqi_ref[...]).astype(BF16)
    kw_ref[...] = _dot(xb, wkw_ref[...])


def _input_proj(x, wts, *, chunk_len, emit_vnorm, seq_len=None):
    t = x.shape[0]
    tm = min(TOKEN_TILE, t)
    assert t % tm == 0 and tm % CHUNK == 0
    row_spec = lambda n, rows_per_token=1: pl.BlockSpec((tm * rows_per_token, n), lambda i: (i, 0))
    out_shapes = [
        jax.ShapeDtypeStruct((t, D_MODEL), BF16),
        jax.ShapeDtypeStruct((t, D_MODEL), BF16),
        jax.ShapeDtypeStruct((t, D_MODEL), BF16),
        jax.ShapeDtypeStruct((t * N_KV_HEADS, HEAD_DIM), F32),
        jax.ShapeDtypeStruct((t * N_KV_HEADS, HEAD_DIM), F32),
        jax.ShapeDtypeStruct((t, KV_DIM), BF16),
        jax.ShapeDtypeStruct((t, KV_DIM), BF16),
        jax.ShapeDtypeStruct((t, N_IDX_HEADS * IDX_DIM), BF16),
        jax.ShapeDtypeStruct((t, LANES), F32),
    ]
    out_specs = [row_spec(s.shape[1], s.shape[0] // t) for s in out_shapes]
    if seq_len is not None:
        assert seq_len % tm == 0 and t % seq_len == 0
        tiles = seq_len // tm
        out_shapes[6] = jax.ShapeDtypeStruct((t // seq_len, KV_DIM, seq_len), BF16)
        out_specs[6] = pl.BlockSpec((1, KV_DIM, tm), lambda i: (i // tiles, 0, i % tiles))
    if emit_vnorm:
        out_shapes.append(jax.ShapeDtypeStruct((t, D_MODEL), F32))
        out_specs.append(row_spec(D_MODEL))
    weights = (wts["wuv"], wts["wq"], wts["wkv"], wts["wqi"], wts["wkw"], wts["wg"], wts["ln_v_g"],
               wts["ln_v_b"], wts["wmix"], wts["bmix"], wts["wpa"])
    return pl.pallas_call(
        functools.partial(_input_proj_kernel, chunk_len=chunk_len, emit_vnorm=emit_vnorm,
                          transpose_v=seq_len is not None,
                          q_scale=HEAD_DIM ** -0.5 * (LOG2_E if seq_len is not None else 1.0)),
        grid=(t // tm,),
        in_specs=[row_spec(D_MODEL)] + [_resident(w.shape) for w in weights],
        out_specs=out_specs,
        out_shape=out_shapes,
        scratch_shapes=[pltpu.VMEM((tm, D_MODEL), BF16)],
        compiler_params=pltpu.CompilerParams(dimension_semantics=("arbitrary",),
                                             vmem_limit_bytes=VMEM_LIMIT_BYTES),
        name="input_proj_sample" if emit_vnorm else "input_proj_prompt",
    )(x, *weights)


def _merge_ffn_kernel(x_ref, a_ref, sgb_ref, bo_ref, wpb_ref, wo_ref, ln1g_ref, ln1b_ref, wgate_ref,
                      wup_ref, wdown_ref, ln2g_ref, ln2b_ref, y_ref, *, alpha):
    pb = _dot(bo_ref[...], wpb_ref[...])
    merged = a_ref[...].astype(F32) + sgb_ref[...].astype(F32) * pb
    mix = _dot(merged.astype(BF16), wo_ref[...])
    x1 = _layer_norm(alpha * x_ref[...] + mix, ln1g_ref[...], ln1b_ref[...])
    x1b = x1.astype(BF16)
    hg = _dot(x1b, wgate_ref[...])
    hu = _dot(x1b, wup_ref[...])
    h = (hg * _sigmoid(hg)) * hu
    f = _dot(h.astype(BF16), wdown_ref[...])
    y_ref[...] = _layer_norm(alpha * x1 + f, ln2g_ref[...], ln2b_ref[...])


def _merge_ffn(x, a, sgb, bo, wts, *, alpha, name):
    t = x.shape[0]
    tm = min(TOKEN_TILE, t)
    assert t % tm == 0
    row_spec = pl.BlockSpec((tm, D_MODEL), lambda i: (i, 0))
    weights = (wts["wpb"], wts["wo"], wts["ln1_g"], wts["ln1_b"], wts["wgate"], wts["wup"], wts["wdown"],
               wts["ln2_g"], wts["ln2_b"])
    return pl.pallas_call(
        functools.partial(_merge_ffn_kernel, alpha=alpha),
        grid=(t // tm,),
        in_specs=[row_spec] * 4 + [_resident(w.shape) for w in weights],
        out_specs=row_spec,
        out_shape=jax.ShapeDtypeStruct((t, D_MODEL), F32),
        compiler_params=pltpu.CompilerParams(dimension_semantics=("arbitrary",),
                                             vmem_limit_bytes=VMEM_LIMIT_BYTES),
        name=name,
    )(x, a, sgb, bo, *weights)


def _lane_partial(x):
    acc = x[:, :LANES]
    for i in range(1, x.shape[1] // LANES):
        acc = acc + x[:, i * LANES:(i + 1) * LANES]
    return acc


def _kth_largest_key(count_ge, shape, topk):
    def bit_body(i, carry):
        u, cnt_u = carry
        cand = u | lax.shift_left(jnp.int32(1), 31 - i)
        cnt = count_ge(_key_to_float(cand))
        take = cnt >= topk
        return jnp.where(take, cand, u), jnp.where(take, cnt, cnt_u)
    return lax.fori_loop(0, 32, bit_body, (jnp.zeros(shape, jnp.int32), jnp.zeros(shape, F32)))


def _sublane_partial(x):
    parts = [x[i * SUBLANES:(i + 1) * SUBLANES] for i in range(x.shape[0] // SUBLANES)]
    while len(parts) > 1:
        parts = [a + b for a, b in zip(parts[::2], parts[1::2])] + parts[len(parts) - len(parts) % 2:]
    return parts[0]


def _select_chunk_t(s, thr, need, carry, tril):
    gt = s > thr
    eq = s == thr
    eqf = eq.astype(F32)
    inc = _dot(tril, eqf.astype(BF16))
    before = carry + inc - eqf
    sel = gt | (eq & (before < need))
    return jnp.where(sel, 0.0, NEG_INF), carry + inc[inc.shape[0] - 1:]


def _threshold_bias(s, thr):
    return jnp.where(s >= thr, 0.0, NEG_INF)


def _select_chunk(s, thr, need, carry, tri):
    gt = s > thr
    eq = s == thr
    eqf = eq.astype(F32)
    inc = _dot(eqf.astype(BF16), tri)
    before = carry + inc - eqf
    sel = gt | (eq & (before < need))
    bias = jnp.where(sel, 0.0, NEG_INF)
    return bias, carry + inc[:, inc.shape[1] - 1:]


def _prompt_attn_kernel(q_ref, qi_ref, kwq_ref, kb_ref, vt_ref, kw_ref, tril_ref, o_ref,
                        sc_ref, m_ref, acc_ref, *, topk):
    tq = q_ref.shape[1]
    ck = KEY_CHUNK
    j = pl.program_id(1)
    n_chunks = lax.shift_right_logical((j + 1) * tq + (ck - 1), int(math.log2(ck)))
    t = j * tq + lax.broadcasted_iota(jnp.int32, (1, tq), 1)
    chunk = lambda c: pl.ds(pl.multiple_of(c * ck, ck), ck)

    qi = qi_ref[0]
    w_t = kwq_ref[0].T[IDX_DIM:IDX_DIM + N_IDX_HEADS]
    w_t = (w_t * (N_IDX_HEADS ** -0.5)) * (IDX_DIM ** -0.5)
    qi_pairs = [jnp.concatenate([qi[:, h * IDX_DIM:(h + 1) * IDX_DIM] for h in (2 * p, 2 * p + 1)], axis=0)
                for p in range(N_IDX_HEADS // 2)]
    w_pairs = [jnp.concatenate([w_t[h:h + 1] for h in (2 * p, 2 * p + 1)], axis=1)
               for p in range(N_IDX_HEADS // 2)]

    def score_chunk(c, carry):
        kc = kw_ref[0, chunk(c), :][:, :IDX_DIM].astype(BF16)
        acc = jnp.zeros((ck, tq), F32)
        for p in range(N_IDX_HEADS // 2):
            d = jnp.maximum(_dot_nt(kc, qi_pairs[p]), 0.0) * w_pairs[p]
            acc = acc + (d[:, :tq] + d[:, tq:])
        kpos = c * ck + lax.broadcasted_iota(jnp.int32, (ck, 1), 0)
        sc_ref[chunk(c), :] = jnp.where(kpos <= t, acc, NEG_INF)
        return carry

    lax.fori_loop(0, n_chunks, score_chunk, 0)

    def count(pred):
        n_acc = 4

        def body(c, accs):
            accs = list(accs)
            for i in range(ck // SUBLANES):
                rows = pl.ds(pl.multiple_of(c * ck + i * SUBLANES, SUBLANES), SUBLANES)
                accs[i % n_acc] = accs[i % n_acc] + pred(sc_ref[rows, :]).astype(F32)
            return tuple(accs)

        zero = jnp.zeros((SUBLANES, tq), F32)
        accs = lax.fori_loop(0, n_chunks, body, (zero,) * n_acc)
        return jnp.sum((accs[0] + accs[1]) + (accs[2] + accs[3]), axis=0, keepdims=True)

    u, cnt_u = _kth_largest_key(lambda cand: count(lambda s: s >= cand), (1, tq), topk)
    small = t < topk
    thr = jnp.where(small, F32_LOWEST, _key_to_float(u))
    has_ties = jnp.max(jnp.where(small, 0.0, cnt_u - topk)) > 0.0

    @pl.when(has_ties)
    def _():
        need = topk - count(lambda s: s > thr)

        def select_chunk(c, carry):
            bias, carry = _select_chunk_t(sc_ref[chunk(c), :], thr, need, carry, tril_ref[...])
            sc_ref[chunk(c), :] = bias
            return carry

        lax.fori_loop(0, n_chunks, select_chunk, jnp.zeros((1, tq), F32))

    @pl.when(jnp.logical_not(has_ties))
    def _():
        def select_chunk(c, carry):
            sc_ref[chunk(c), :] = _threshold_bias(sc_ref[chunk(c), :], thr)
            return carry

        lax.fori_loop(0, n_chunks, select_chunk, 0)

    q = q_ref[0]
    ones = jnp.ones((BF16_SUBLANES, ck), BF16)
    q2s = [jnp.concatenate([q[:, (g * Q_PER_KV + i) * HEAD_DIM:(g * Q_PER_KV + i + 1) * HEAD_DIM]
                            for i in range(Q_PER_KV)], axis=0) for g in range(N_KV_HEADS)]
    m_ref[...] = jnp.full(m_ref.shape, M_INIT, F32)
    acc_ref[...] = jnp.zeros(acc_ref.shape, F32)

    def attend_chunk(c, carry):
        bias = jnp.concatenate([sc_ref[chunk(c), :]] * Q_PER_KV, axis=1)
        for g in range(N_KV_HEADS):
            gs = slice(g * HEAD_DIM, (g + 1) * HEAD_DIM)
            s = _dot_nt(kb_ref[0, chunk(c), gs], q2s[g]) + bias
            m_old = m_ref[g][:1]
            m_new = jnp.maximum(m_old, jnp.max(s, axis=0, keepdims=True))
            p = jnp.exp2(s - m_new).astype(BF16)
            v1 = jnp.concatenate([vt_ref[0, gs, chunk(c)], ones], axis=0)
            acc_ref[g] = jnp.exp2(m_old - m_new) * acc_ref[g] + _dot(v1, p)
            m_ref[g] = jnp.broadcast_to(m_new, m_ref.shape[1:])
        return carry

    lax.fori_loop(0, n_chunks, attend_chunk, 0)
    for g in range(N_KV_HEADS):
        acc = acc_ref[g]
        o_t = acc[:HEAD_DIM] / acc[HEAD_DIM:HEAD_DIM + 1]
        for i in range(Q_PER_KV):
            h = g * Q_PER_KV + i
            o_ref[0, :, h * HEAD_DIM:(h + 1) * HEAD_DIM] = o_t[:, i * tq:(i + 1) * tq].T.astype(BF16)


def _prompt_attn(q, qi, kw, kb, vt, tril):
    b, s, _ = q.shape
    tq = Q_TILE
    assert s % KEY_CHUNK == 0 and s % tq == 0 and tq % LANES == 0
    topk = min(TOPK_MAX, s // 4)
    blk = lambda n: pl.BlockSpec((1, tq, n), lambda i, j: (i, j, 0))
    seq = lambda n: pl.BlockSpec((1, s, n), lambda i, j: (i, 0, 0))
    return pl.pallas_call(
        functools.partial(_prompt_attn_kernel, topk=topk),
        grid=(b, s // tq),
        in_specs=[blk(D_MODEL), blk(N_IDX_HEADS * IDX_DIM), blk(LANES), seq(KV_DIM),
                  pl.BlockSpec((1, KV_DIM, s), lambda i, j: (i, 0, 0)), seq(LANES), _resident(tril.shape)],
        out_specs=blk(D_MODEL),
        out_shape=jax.ShapeDtypeStruct((b, s, D_MODEL), BF16),
        scratch_shapes=[pltpu.VMEM((s, tq), F32),
                        pltpu.VMEM((N_KV_HEADS, SUBLANES, Q_PER_KV * tq), F32),
                        pltpu.VMEM((N_KV_HEADS, HEAD_DIM + BF16_SUBLANES, Q_PER_KV * tq), F32)],
        compiler_params=pltpu.CompilerParams(dimension_semantics=("arbitrary", "arbitrary"),
                                             vmem_limit_bytes=VMEM_LIMIT_BYTES),
        name="prompt_attn",
    )(q, qi, kw, kb, vt, kw, tril)


def _sample_score_kernel(pt_ref, qi_ref, w_ref, knew_ref, *refs, n_pages, t_new):
    del pt_ref
    page_refs = refs[:n_pages]
    sp_ref, sn_ref = refs[n_pages:]
    qi = qi_ref[0]
    w = (w_ref[0] * (N_IDX_HEADS ** -0.5)) * (IDX_DIM ** -0.5)

    def head_sum(dots):
        r = jnp.maximum(dots, 0.0) * w
        return [jnp.sum(r[i * N_IDX_HEADS:(i + 1) * N_IDX_HEADS], axis=0, keepdims=True)
                for i in range(t_new)]

    group = 4
    for p0 in range(0, n_pages, group):
        keys_t = jnp.concatenate([page_refs[p][0] for p in range(p0, p0 + group)], axis=1).astype(BF16)
        for i, r in enumerate(head_sum(_dot(qi, keys_t))):
            sp_ref[0, i:i + 1, p0 * PAGE_SIZE:(p0 + group) * PAGE_SIZE] = r

    knew = jnp.concatenate([knew_ref[0][:, :IDX_DIM],
                            jnp.zeros((LANES - knew_ref.shape[1], IDX_DIM), F32)], axis=0).astype(BF16)
    kpos = lax.broadcasted_iota(jnp.int32, (1, LANES), 1)
    for i, r in enumerate(head_sum(_dot_nt(qi, knew))):
        sn_ref[0, i:i + 1, :] = jnp.where(kpos <= i, r, NEG_INF)


def _sample_scores(page_table, qi32, w32, knew, cache_kidx_t):
    db, n_pages = page_table.shape
    t_new = qi32.shape[1] // N_IDX_HEADS
    assert n_pages % 4 == 0
    page_spec = lambda p: pl.BlockSpec((1, IDX_DIM, PAGE_SIZE), lambda b, pt: (pt[b, p], 0, 0))
    grid_spec = pltpu.PrefetchScalarGridSpec(
        num_scalar_prefetch=1,
        grid=(db,),
        in_specs=[pl.BlockSpec((1,) + qi32.shape[1:], lambda b, pt: (b, 0, 0)),
                  pl.BlockSpec((1,) + w32.shape[1:], lambda b, pt: (b, 0, 0)),
                  pl.BlockSpec((1,) + knew.shape[1:], lambda b, pt: (b, 0, 0))]
                 + [page_spec(p) for p in range(n_pages)],
        out_specs=[pl.BlockSpec((1, t_new, n_pages * PAGE_SIZE), lambda b, pt: (b, 0, 0)),
                   pl.BlockSpec((1, t_new, LANES), lambda b, pt: (b, 0, 0))],
    )
    return pl.pallas_call(
        functools.partial(_sample_score_kernel, n_pages=n_pages, t_new=t_new),
        grid_spec=grid_spec,
        out_shape=[jax.ShapeDtypeStruct((db, t_new, n_pages * PAGE_SIZE), F32),
                   jax.ShapeDtypeStruct((db, t_new, LANES), F32)],
        compiler_params=pltpu.CompilerParams(dimension_semantics=("arbitrary",),
                                             vmem_limit_bytes=VMEM_LIMIT_BYTES),
        name="sample_scores",
    )(page_table, qi32, w32, knew, *([cache_kidx_t] * n_pages))


def _sample_select_kernel(sp_ref, sn_ref, tri_ref, slot_ref, bn_ref, npast_ref, *, topk):
    rows, past = sp_ref.shape
    ck = KEY_CHUNK
    n_chunks = past // ck

    def count(pred):
        def body(c, cnt):
            s = sp_ref[:, pl.ds(pl.multiple_of(c * ck, ck), ck)]
            return cnt + _lane_partial(pred(s).astype(F32))
        part = lax.fori_loop(0, n_chunks, body, pred(sn_ref[...]).astype(F32))
        return jnp.sum(part, axis=-1, keepdims=True)

    u, _ = _kth_largest_key(lambda cand: count(lambda s: s >= cand), (rows, 1), topk)
    thr = _key_to_float(u)
    need = topk - count(lambda s: s > thr)

    def select_chunk(c, carry):
        ties, taken = carry
        cs = pl.ds(pl.multiple_of(c * ck, ck), ck)
        bias, ties = _select_chunk(sp_ref[:, cs], thr, need, ties, tri_ref[...])
        sel = (bias == 0.0).astype(F32)
        inc = _dot(sel.astype(BF16), tri_ref[...])
        slot_ref[:, cs] = jnp.where(sel > 0.0, taken + inc - 1.0, -1.0).astype(jnp.int32)
        return ties, taken + inc[:, ck - 1:]

    zero = jnp.zeros((rows, 1), F32)
    ties, taken = lax.fori_loop(0, n_chunks, select_chunk, (zero, zero))
    bias, _ = _select_chunk(sn_ref[...], thr, need, ties, tri_ref[:LANES, :LANES])
    bn_ref[...] = bias
    npast_ref[...] = jnp.broadcast_to(taken, npast_ref.shape)


def _sample_select(sp, sn, tri, *, topk):
    n, past = sp.shape
    rows = min(SAMPLE_ROWS, n)
    assert n % rows == 0 and past % KEY_CHUNK == 0
    lane_spec = pl.BlockSpec((rows, LANES), lambda i: (i, 0))
    return pl.pallas_call(
        functools.partial(_sample_select_kernel, topk=topk),
        grid=(n // rows,),
        in_specs=[pl.BlockSpec((rows, past), lambda i: (i, 0)), lane_spec, _resident(tri.shape)],
        out_specs=[pl.BlockSpec((rows, past), lambda i: (i, 0)), lane_spec, lane_spec],
        out_shape=[jax.ShapeDtypeStruct((n, past), jnp.int32), jax.ShapeDtypeStruct((n, LANES), F32),
                   jax.ShapeDtypeStruct((n, LANES), F32)],
        compiler_params=pltpu.CompilerParams(dimension_semantics=("arbitrary",),
                                             vmem_limit_bytes=VMEM_LIMIT_BYTES),
        name="sample_select",
    )(sp, sn, tri)


def _sample_gather(slot, page_table, table_k, table_v, *, topk):
    nq, n_pages, _ = slot.shape
    db = page_table.shape[0]
    t_new = nq // db
    workers = SC_CORES * SC_SUBCORES
    assert nq % workers == 0 and topk % SC_GATHER_WINDOW == 0 and PAGE_SIZE % SC_LANES == 0
    assert t_new & (t_new - 1) == 0
    per_worker = nq // workers
    row_shape = (N_KV_HEADS, HEAD_DIM)
    out = jax.ShapeDtypeStruct((nq, topk) + row_shape, F32)
    mesh = plsc.VectorSubcoreMesh(core_axis_name="c", subcore_axis_name="s")

    @functools.partial(
        pl.kernel, mesh=mesh, out_type=(out, out),
        scratch_types=[pltpu.VMEM((n_pages, PAGE_SIZE), jnp.int32),
                       pltpu.VMEM((n_pages,), jnp.int32),
                       pltpu.VMEM((topk,), jnp.int32),
                       pltpu.VMEM((SC_GATHER_WINDOW,) + row_shape, F32)],
        compiler_params=pltpu.CompilerParams(needs_layout_passes=False),
        name="sample_gather")
    def gather(slot_hbm, pt_hbm, tk_hbm, tv_hbm, kg_hbm, vg_hbm, slot_v, pt_v, idx_v, rows_v):
        worker = lax.axis_index("c") * SC_SUBCORES + lax.axis_index("s")
        lane = lax.iota(jnp.int32, SC_LANES)

        @pl.loop(0, per_worker)
        def _(i):
            q = worker * per_worker + i
            pltpu.sync_copy(slot_hbm.at[q], slot_v)
            pltpu.sync_copy(pt_hbm.at[lax.shift_right_logical(q, int(math.log2(t_new)))], pt_v)

            @pl.loop(0, topk // SC_LANES)
            def _(j):
                idx_v[pl.ds(j * SC_LANES, SC_LANES)] = jnp.zeros((SC_LANES,), jnp.int32)

            @pl.loop(0, n_pages)
            def _(p):
                base = plsc.load_gather(pt_v, [jnp.full((SC_LANES,), p, jnp.int32)]) * PAGE_SIZE
                for l in range(PAGE_SIZE // SC_LANES):
                    s = slot_v[p, pl.ds(l * SC_LANES, SC_LANES)]
                    plsc.store_scatter(idx_v, [s], base + (l * SC_LANES + lane), mask=s >= 0)

            for w in range(topk // SC_GATHER_WINDOW):
                win = pl.ds(w * SC_GATHER_WINDOW, SC_GATHER_WINDOW)
                for table, dst in ((tk_hbm, kg_hbm), (tv_hbm, vg_hbm)):
                    pltpu.sync_copy(table.at[idx_v.at[win]], rows_v)
                    pltpu.sync_copy(rows_v, dst.at[q, win])

    return gather(slot, page_table, table_k, table_v)


def _sample_attn_kernel(q_ref, kg_ref, vg_ref, knew_ref, vnew_ref, bn_ref, npast_ref, o_ref, *, t_new, topk):
    rows = t_new * Q_PER_KV
    n_keys = t_new * topk
    row_query = lax.shift_right_logical(lax.broadcasted_iota(jnp.int32, (rows, 1), 0), int(math.log2(Q_PER_KV)))
    col = lax.broadcasted_iota(jnp.int32, (1, n_keys), 1)
    col_query = lax.shift_right_logical(col, int(math.log2(topk)))
    col_slot = (col & (topk - 1)).astype(F32)
    own = (col_query == row_query) & (col_slot < npast_ref[0][:, :1])
    bias_past = jnp.where(own, 0.0, NEG_INF)
    bias_new = bn_ref[0]
    for h in range(N_KV_HEADS):
        hs = slice(h * HEAD_DIM, (h + 1) * HEAD_DIM)
        q = q_ref[0, h]
        keys = kg_ref[pl.ds(h, n_keys, stride=N_KV_HEADS), :].astype(BF16)
        vals = vg_ref[pl.ds(h, n_keys, stride=N_KV_HEADS), :].astype(BF16)
        s_past = _dot_nt(q, keys) + bias_past
        s_new = _dot_nt(q, knew_ref[0][:, hs]) + bias_new
        m = jnp.maximum(jnp.max(s_past, axis=-1, keepdims=True), jnp.max(s_new, axis=-1, keepdims=True))
        p_past = jnp.exp(s_past - m)
        p_new = jnp.exp(s_new - m)
        denom = jnp.sum(p_past, axis=-1, keepdims=True) + jnp.sum(p_new, axis=-1, keepdims=True)
        o = _dot(p_past.astype(BF16), vals) + _dot(p_new.astype(BF16), vnew_ref[0][:, hs])
        o_ref[0, h] = (o / denom).astype(BF16)


def _sample_attn(q8, kg, vg, knew, vnew, bias_new8, npast8, *, topk):
    db, _, rows, _ = q8.shape
    t_new = rows // Q_PER_KV
    assert topk & (topk - 1) == 0
    gathered = pl.BlockSpec((t_new * topk * N_KV_HEADS, HEAD_DIM), lambda b: (b, 0))
    per_seq = lambda arr: pl.BlockSpec((1,) + arr.shape[1:], lambda b: (b,) + (0,) * (arr.ndim - 1))
    return pl.pallas_call(
        functools.partial(_sample_attn_kernel, t_new=t_new, topk=topk),
        grid=(db,),
        in_specs=[per_seq(q8), gathered, gathered, per_seq(knew), per_seq(vnew), per_seq(bias_new8),
                  per_seq(npast8)],
        out_specs=per_seq(q8),
        out_shape=jax.ShapeDtypeStruct(q8.shape, BF16),
        compiler_params=pltpu.CompilerParams(dimension_semantics=("arbitrary",),
                                             vmem_limit_bytes=VMEM_LIMIT_BYTES),
        name="sample_attn",
    )(q8, kg, vg, knew, vnew, bias_new8, npast8)


def _sample_mixer_b(page_table, q, qi, kw, kb, vb, cache_k, cache_v, cache_kidx, tri):
    db, n_pages = page_table.shape
    n_pool = cache_k.shape[0]
    t_new = q.shape[0] // db
    past = n_pages * PAGE_SIZE
    topk = min(TOPK_MAX, (past + t_new) // 4)
    rows = t_new * N_IDX_HEADS
    qi32 = qi.reshape(db, rows, IDX_DIM)
    w32 = kw[:, IDX_DIM:IDX_DIM + N_IDX_HEADS].reshape(db, rows, 1)
    knew_idx = jnp.pad(kw.reshape(db, t_new, LANES), ((0, 0), (0, SUBLANES - t_new), (0, 0)))
    sp, sn = _sample_scores(page_table, qi32, w32, knew_idx, jnp.swapaxes(cache_kidx, 1, 2))
    slot, bias_n, npast = _sample_select(sp.reshape(db * t_new, past), sn.reshape(db * t_new, LANES), tri,
                                         topk=topk)
    kg, vg = _sample_gather(slot.reshape(db * t_new, n_pages, PAGE_SIZE), page_table,
                            cache_k.reshape(n_pool * PAGE_SIZE, N_KV_HEADS, HEAD_DIM),
                            cache_v.reshape(n_pool * PAGE_SIZE, N_KV_HEADS, HEAD_DIM), topk=topk)
    q8 = q.reshape(db, t_new, N_KV_HEADS, Q_PER_KV, HEAD_DIM).transpose(0, 2, 1, 3, 4)
    q8 = q8.reshape(db, N_KV_HEADS, t_new * Q_PER_KV, HEAD_DIM)
    per_row = lambda arr: jnp.repeat(arr.reshape(db, t_new, LANES), Q_PER_KV, axis=1)
    pad_new = lambda arr: jnp.pad(arr.reshape(db, t_new, KV_DIM), ((0, 0), (0, NEW_ROWS - t_new), (0, 0)))
    flat = lambda arr: arr.reshape(db * t_new * topk * N_KV_HEADS, HEAD_DIM)
    o8 = _sample_attn(q8, flat(kg), flat(vg), pad_new(kb), pad_new(vb), per_row(bias_n), per_row(npast),
                      topk=topk)
    o = o8.reshape(db, N_KV_HEADS, t_new, Q_PER_KV, HEAD_DIM).transpose(0, 2, 1, 3, 4)
    return o.reshape(db * t_new, D_MODEL)


def _prepare_weights(layer, w_in, ln_v_g, ln_v_b, w_s, b_s, w_pa, w_pb, w_o, ln1_g, ln1_b, w_gate, w_up,
                     w_down, ln2_g, ln2_b, t_new):
    w = w_in[layer]
    o_q = 2 * D_MODEL
    o_kv = o_q + D_MODEL
    o_qi = o_kv + 2 * KV_DIM
    o_ki = o_qi + N_IDX_HEADS * IDX_DIM
    o_g = o_ki + IDX_DIM + N_IDX_HEADS
    row = lambda v: v[layer].reshape(1, -1)
    reps = CHUNK // t_new
    shared = dict(
        wuv=w[:, :o_q].astype(BF16), wq=w[:, o_q:o_kv].astype(BF16), wkv=w[:, o_kv:o_qi].astype(BF16),
        wqi=w[:, o_qi:o_ki].astype(BF16),
        wkw=jnp.pad(w[:, o_ki:o_g], ((0, 0), (0, LANES - (o_g - o_ki)))).astype(BF16),
        wg=w[:, o_g:].astype(BF16),
        ln_v_g=row(ln_v_g), ln_v_b=row(ln_v_b), wpa=w_pa[layer].astype(BF16),
        wpb=w_pb[layer].astype(BF16), wo=w_o[layer].astype(BF16), ln1_g=row(ln1_g), ln1_b=row(ln1_b),
        wgate=w_gate[layer].astype(BF16), wup=w_up[layer].astype(BF16), wdown=w_down[layer].astype(BF16),
        ln2_g=row(ln2_g), ln2_b=row(ln2_b))
    prompt = dict(shared, wmix=w_s[layer].astype(BF16),
                  bmix=jnp.repeat(b_s[layer].T, A_GROUP_DIM, axis=1))
    sample = dict(shared, wmix=jnp.tile(w_s[layer][:, :t_new, :t_new], (1, reps, reps)).astype(BF16),
                  bmix=jnp.repeat(jnp.tile(b_s[layer][:, :t_new].T, (reps, 1)), A_GROUP_DIM, axis=1))
    return prompt, sample


def kernel(x_prompt, x_sample, cache_k, cache_v, cache_kidx, page_table, w_in, ln_v_g, ln_v_b, w_s, b_s,
           w_pa, w_pb, w_o, ln1_g, ln1_b, w_gate, w_up, w_down, ln2_g, ln2_b):
    depth = w_in.shape[0]
    alpha = (2 * depth) ** 0.25
    b, s, _ = x_prompt.shape
    db, t_new, _ = x_sample.shape
    n_pages = page_table.shape[1]
    n_pool = cache_k.shape[1]
    past = n_pages * PAGE_SIZE
    assert t_new <= CHUNK and CHUNK % t_new == 0 and t_new <= NEW_ROWS
    tri = jnp.triu(jnp.ones((KEY_CHUNK, KEY_CHUNK), BF16))
    tril = jnp.tril(jnp.ones((KEY_CHUNK, KEY_CHUNK), BF16))

    y_p = x_prompt.reshape(b * s, D_MODEL)
    y_s = x_sample.reshape(db * t_new, D_MODEL)
    outs = [[] for _ in range(7)]
    for layer in range(depth):
        wp, ws = _prepare_weights(layer, w_in, ln_v_g, ln_v_b, w_s, b_s, w_pa, w_pb, w_o, ln1_g, ln1_b,
                                  w_gate, w_up, w_down, ln2_g, ln2_b, t_new)

        a, sgb, q, k_p, v_p, kb, vt, qi, kw = _input_proj(y_p, wp, chunk_len=CHUNK, emit_vnorm=False, seq_len=s)
        seq = lambda arr: arr.reshape(b, s, arr.shape[-1])
        bo = _prompt_attn(seq(q), seq(qi), seq(kw), seq(kb), vt, tril)
        y_p = _merge_ffn(y_p, a, sgb, bo.reshape(b * s, D_MODEL), wp, alpha=alpha, name="merge_ffn_prompt")
        outs[0].append(k_p.reshape(b, s, N_KV_HEADS, HEAD_DIM))
        outs[1].append(v_p.reshape(b, s, N_KV_HEADS, HEAD_DIM))
        outs[2].append(kw[:, :IDX_DIM].reshape(b, s, IDX_DIM))

        a, sgb, q, k_s, v_s, kb, vb, qi, kw, vn = _input_proj(y_s, ws, chunk_len=t_new, emit_vnorm=True)
        bo = _sample_mixer_b(page_table, q, qi, kw, kb, vb, cache_k[layer], cache_v[layer], cache_kidx[layer],
                             tri)
        y_s = _merge_ffn(y_s, a, sgb, bo, ws, alpha=alpha, name="merge_ffn_sample")
        outs[3].append(k_s.reshape(db, t_new, N_KV_HEADS, HEAD_DIM))
        outs[4].append(v_s.reshape(db, t_new, N_KV_HEADS, HEAD_DIM))
        outs[5].append(kw[:, :IDX_DIM].reshape(db, t_new, IDX_DIM))
        outs[6].append(vn.reshape(db, t_new, D_MODEL))

    return (y_p.reshape(b, s, D_MODEL), y_s.reshape(db, t_new, D_MODEL), *[jnp.stack(o) for o in outs])
```

```python
import functools
import math

import jax
import jax.numpy as jnp
from jax import lax
from jax.experimental import pallas as pl
from jax.experimental.pallas import tpu as pltpu
from jax.experimental.pallas import tpu_sc as plsc

D_MODEL = 1024
CHUNK = 128
A_GROUPS = 8
A_GROUP_DIM = D_MODEL // A_GROUPS
N_HEADS = 8
N_KV_HEADS = 4
Q_PER_KV = N_HEADS // N_KV_HEADS
HEAD_DIM = D_MODEL // N_HEADS
KV_DIM = N_KV_HEADS * HEAD_DIM
N_IDX_HEADS = 8
IDX_DIM = 64
TOPK_MAX = 256
PAGE_SIZE = 128
LN_EPS = 1e-5

LANES = 128
SUBLANES = 8
BF16_SUBLANES = 16
VMEM_LIMIT_BYTES = 56 * 1024 * 1024

TOKEN_TILE = 512
Q_TILE = 512
KEY_CHUNK = 512
SAMPLE_ROWS = 64
SC_CORES = 2
SC_SUBCORES = 16
SC_LANES = 16
SC_GATHER_WINDOW = 64
NEW_ROWS = 128

F32 = jnp.float32
BF16 = jnp.bfloat16
NEG_INF = float("-inf")
F32_LOWEST = float(jnp.finfo(jnp.float32).min)
M_INIT = -1e30
LOG2_E = math.log2(math.e)
BOUND_SLACK = 1.01
BOUND_FLOOR = 1e-6
BF16_ROUND_UP = 1.0 + 2.0 ** -7
DENOM_FLOOR = 1e-30
INT32_MIN = -(2 ** 31)


def _dot(a, b):
    return jnp.dot(a, b, preferred_element_type=F32)


def _dot_nt(a, b):
    return lax.dot_general(a, b, (((1,), (1,)), ((), ())), preferred_element_type=F32)


def _gelu(x):
    c = math.sqrt(2.0 / math.pi)
    return x * (0.5 * (1.0 + jnp.tanh(c * (x + 0.044715 * (x * x * x)))))


def _sigmoid(x):
    return 1.0 / (1.0 + jnp.exp(-x))


def _layer_norm(x, g, b):
    mu = jnp.mean(x, axis=-1, keepdims=True)
    xc = x - mu
    var = jnp.mean(xc * xc, axis=-1, keepdims=True)
    return xc * lax.rsqrt(var + LN_EPS) * g + b


def _key_to_float(u):
    c = u ^ jnp.int32(INT32_MIN)
    bits = c ^ ((c >> 31) & jnp.int32(0x7FFFFFFF))
    return lax.bitcast_convert_type(bits, F32)


def _resident(shape):
    nd = len(shape)
    return pl.BlockSpec(shape, lambda *_: (0,) * nd, pipeline_mode=pl.Buffered(1))


def _input_proj_kernel(x_ref, wuv_ref, wq_ref, wkv_ref, wqi_ref, wkw_ref, wg_ref, lng_ref, lnb_ref,
                       wmix_ref, bmix_ref, wpa_ref,
                       a_ref, sgb_ref, q_ref, k_ref, v_ref, kb_ref, vb_ref, qi_ref, kw_ref, kn_ref, *rest,
                       chunk_len, emit_vnorm, transpose_v, q_scale):
    if emit_vnorm:
        vn_ref, aout_ref = rest
    else:
        (aout_ref,) = rest
    tm = x_ref.shape[0]
    xb = x_ref[...].astype(BF16)

    zuv = _dot(xb, wuv_ref[...])
    u = _gelu(zuv[:, :D_MODEL])
    vn = _layer_norm(_gelu(zuv[:, D_MODEL:]), lng_ref[...], lnb_ref[...])
    if emit_vnorm:
        vn_ref[...] = vn
    vnb = vn.astype(BF16)

    row = lax.broadcasted_iota(jnp.int32, (CHUNK, CHUNK), 0)
    col = lax.broadcasted_iota(jnp.int32, (CHUNK, CHUNK), 1)
    same_chunk = (row & ~(chunk_len - 1)) == (col & ~(chunk_len - 1))
    mix_mask = (col <= row) & same_chunk
    for g in range(A_GROUPS):
        gs = slice(g * A_GROUP_DIM, (g + 1) * A_GROUP_DIM)
        wm = jnp.where(mix_mask, wmix_ref[g], jnp.zeros((), BF16))
        for c in range(tm // CHUNK):
            rs = slice(c * CHUNK, (c + 1) * CHUNK)
            s = _dot(wm, vnb[rs, gs]) + bmix_ref[:, gs]
            aout_ref[rs, gs] = (u[rs, gs] * s).astype(BF16)

    pa = _dot(aout_ref[...], wpa_ref[...])
    gates = _dot(xb, wg_ref[...])
    a_ref[...] = (_sigmoid(gates[:, :D_MODEL]) * pa).astype(BF16)
    sgb_ref[...] = _sigmoid(gates[:, D_MODEL:]).astype(BF16)

    q_ref[...] = (_dot(xb, wq_ref[...]) * q_scale).astype(BF16)
    kv = _dot(xb, wkv_ref[...])
    for h in range(N_KV_HEADS):
        k_ref[pl.ds(h, tm, stride=N_KV_HEADS), :] = kv[:, h * HEAD_DIM:(h + 1) * HEAD_DIM]
        v_ref[pl.ds(h, tm, stride=N_KV_HEADS), :] = kv[:, KV_DIM + h * HEAD_DIM:KV_DIM + (h + 1) * HEAD_DIM]
    kb = kv[:, :KV_DIM].astype(BF16)
    kb_ref[...] = kb
    ksq = kb.astype(F32)
    ksq = ksq * ksq
    lane = lax.broadcasted_iota(jnp.int32, (tm, LANES), 1)
    kn = jnp.zeros((tm, LANES), F32)
    for h in range(N_KV_HEADS):
        kn = jnp.where(lane == h, jnp.sum(ksq[:, h * HEAD_DIM:(h + 1) * HEAD_DIM], axis=-1, keepdims=True), kn)
    kn_ref[...] = kn
    if transpose_v:
        vb_ref[0] = kv[:, KV_DIM:].T.astype(BF16)
    else:
        vb_ref[...] = kv[:, KV_DIM:].astype(BF16)
    qi_ref[...] = _dot(xb, wqi_ref[...]).astype(BF16)
    kw_ref[...] = _dot(xb, wkw_ref[...])


def _input_proj(x, wts, *, chunk_len, emit_vnorm, seq_len=None):
    t = x.shape[0]
    tm = min(TOKEN_TILE, t)
    assert t % tm == 0 and tm % CHUNK == 0
    row_spec = lambda n, rows_per_token=1: pl.BlockSpec((tm * rows_per_token, n), lambda i: (i, 0))
    out_shapes = [
        jax.ShapeDtypeStruct((t, D_MODEL), BF16),
        jax.ShapeDtypeStruct((t, D_MODEL), BF16),
        jax.ShapeDtypeStruct((t, D_MODEL), BF16),
        jax.ShapeDtypeStruct((t * N_KV_HEADS, HEAD_DIM), F32),
        jax.ShapeDtypeStruct((t * N_KV_HEADS, HEAD_DIM), F32),
        jax.ShapeDtypeStruct((t, KV_DIM), BF16),
        jax.ShapeDtypeStruct((t, KV_DIM), BF16),
        jax.ShapeDtypeStruct((t, N_IDX_HEADS * IDX_DIM), BF16),
        jax.ShapeDtypeStruct((t, LANES), F32),
        jax.ShapeDtypeStruct((t, LANES), F32),
    ]
    out_specs = [row_spec(s.shape[1], s.shape[0] // t) for s in out_shapes]
    if seq_len is not None:
        assert seq_len % tm == 0 and t % seq_len == 0
        tiles = seq_len // tm
        out_shapes[6] = jax.ShapeDtypeStruct((t // seq_len, KV_DIM, seq_len), BF16)
        out_specs[6] = pl.BlockSpec((1, KV_DIM, tm), lambda i: (i // tiles, 0, i % tiles))
    if emit_vnorm:
        out_shapes.append(jax.ShapeDtypeStruct((t, D_MODEL), F32))
        out_specs.append(row_spec(D_MODEL))
    weights = (wts["wuv"], wts["wq"], wts["wkv"], wts["wqi"], wts["wkw"], wts["wg"], wts["ln_v_g"],
               wts["ln_v_b"], wts["wmix"], wts["bmix"], wts["wpa"])
    return pl.pallas_call(
        functools.partial(_input_proj_kernel, chunk_len=chunk_len, emit_vnorm=emit_vnorm,
                          transpose_v=seq_len is not None,
                          q_scale=HEAD_DIM ** -0.5 * (LOG2_E if seq_len is not None else 1.0)),
        grid=(t // tm,),
        in_specs=[row_spec(D_MODEL)] + [_resident(w.shape) for w in weights],
        out_specs=out_specs,
        out_shape=out_shapes,
        scratch_shapes=[pltpu.VMEM((tm, D_MODEL), BF16)],
        compiler_params=pltpu.CompilerParams(dimension_semantics=("arbitrary",),
                                             vmem_limit_bytes=VMEM_LIMIT_BYTES),
        name="input_proj_sample" if emit_vnorm else "input_proj_prompt",
    )(x, *weights)


def _merge_ffn_kernel(x_ref, a_ref, sgb_ref, bo_ref, wpb_ref, wo_ref, ln1g_ref, ln1b_ref, wgate_ref,
                      wup_ref, wdown_ref, ln2g_ref, ln2b_ref, y_ref, *, alpha):
    pb = _dot(bo_ref[...], wpb_ref[...])
    merged = a_ref[...].astype(F32) + sgb_ref[...].astype(F32) * pb
    mix = _dot(merged.astype(BF16), wo_ref[...])
    x1 = _layer_norm(alpha * x_ref[...] + mix, ln1g_ref[...], ln1b_ref[...])
    x1b = x1.astype(BF16)
    hg = _dot(x1b, wgate_ref[...])
    hu = _dot(x1b, wup_ref[...])
    h = (hg * _sigmoid(hg)) * hu
    f = _dot(h.astype(BF16), wdown_ref[...])
    y_ref[...] = _layer_norm(alpha * x1 + f, ln2g_ref[...], ln2b_ref[...])


def _merge_ffn(x, a, sgb, bo, wts, *, alpha, name):
    t = x.shape[0]
    tm = min(TOKEN_TILE, t)
    assert t % tm == 0
    row_spec = pl.BlockSpec((tm, D_MODEL), lambda i: (i, 0))
    weights = (wts["wpb"], wts["wo"], wts["ln1_g"], wts["ln1_b"], wts["wgate"], wts["wup"], wts["wdown"],
               wts["ln2_g"], wts["ln2_b"])
    return pl.pallas_call(
        functools.partial(_merge_ffn_kernel, alpha=alpha),
        grid=(t // tm,),
        in_specs=[row_spec] * 4 + [_resident(w.shape) for w in weights],
        out_specs=row_spec,
        out_shape=jax.ShapeDtypeStruct((t, D_MODEL), F32),
        compiler_params=pltpu.CompilerParams(dimension_semantics=("arbitrary",),
                                             vmem_limit_bytes=VMEM_LIMIT_BYTES),
        name=name,
    )(x, a, sgb, bo, *weights)


def _lane_partial(x):
    acc = x[:, :LANES]
    for i in range(1, x.shape[1] // LANES):
        acc = acc + x[:, i * LANES:(i + 1) * LANES]
    return acc


def _kth_largest_key(count_ge, shape, topk):
    def bit_body(i, carry):
        u, cnt_u = carry
        cand = u | lax.shift_left(jnp.int32(1), 31 - i)
        cnt = count_ge(_key_to_float(cand))
        take = cnt >= topk
        return jnp.where(take, cand, u), jnp.where(take, cnt, cnt_u)
    return lax.fori_loop(0, 32, bit_body, (jnp.zeros(shape, jnp.int32), jnp.zeros(shape, F32)))


def _sublane_partial(x):
    parts = [x[i * SUBLANES:(i + 1) * SUBLANES] for i in range(x.shape[0] // SUBLANES)]
    while len(parts) > 1:
        parts = [a + b for a, b in zip(parts[::2], parts[1::2])] + parts[len(parts) - len(parts) % 2:]
    return parts[0]


def _select_chunk_t(s, thr, need, carry, tril):
    gt = s > thr
    eq = s == thr
    eqf = eq.astype(F32)
    inc = _dot(tril, eqf.astype(BF16))
    before = carry + inc - eqf
    sel = gt | (eq & (before < need))
    return jnp.where(sel, 0.0, NEG_INF), carry + inc[inc.shape[0] - 1:]


def _threshold_bias(s, thr):
    return jnp.where(s >= thr, 0.0, NEG_INF)


def _select_chunk(s, thr, need, carry, tri):
    gt = s > thr
    eq = s == thr
    eqf = eq.astype(F32)
    inc = _dot(eqf.astype(BF16), tri)
    before = carry + inc - eqf
    sel = gt | (eq & (before < need))
    bias = jnp.where(sel, 0.0, NEG_INF)
    return bias, carry + inc[:, inc.shape[1] - 1:]


def _prompt_attn_kernel(q_ref, qi_ref, kwq_ref, kb_ref, kn_ref, vt_ref, kw_ref, tril_ref, o_ref,
                        sc_ref, m_ref, acc_ref, *, topk):
    tq = q_ref.shape[1]
    ck = KEY_CHUNK
    j = pl.program_id(1)
    n_chunks = lax.shift_right_logical((j + 1) * tq + (ck - 1), int(math.log2(ck)))
    t = j * tq + lax.broadcasted_iota(jnp.int32, (1, tq), 1)
    chunk = lambda c: pl.ds(pl.multiple_of(c * ck, ck), ck)

    qi = qi_ref[0]
    w_t = kwq_ref[0].T[IDX_DIM:IDX_DIM + N_IDX_HEADS]
    w_t = (w_t * (N_IDX_HEADS ** -0.5)) * (IDX_DIM ** -0.5)
    qi_pairs = [jnp.concatenate([qi[:, h * IDX_DIM:(h + 1) * IDX_DIM] for h in (2 * p, 2 * p + 1)], axis=0)
                for p in range(N_IDX_HEADS // 2)]
    w_pairs = [jnp.concatenate([w_t[h:h + 1] for h in (2 * p, 2 * p + 1)], axis=1)
               for p in range(N_IDX_HEADS // 2)]

    def score_chunk(c, carry):
        kc = kw_ref[0, chunk(c), :][:, :IDX_DIM].astype(BF16)
        acc = jnp.zeros((ck, tq), F32)
        for p in range(N_IDX_HEADS // 2):
            d = jnp.maximum(_dot_nt(kc, qi_pairs[p]), 0.0) * w_pairs[p]
            acc = acc + (d[:, :tq] + d[:, tq:])
        kpos = c * ck + lax.broadcasted_iota(jnp.int32, (ck, 1), 0)
        sc_ref[chunk(c), :] = jnp.where(kpos <= t, acc, NEG_INF)
        return carry

    lax.fori_loop(0, n_chunks, score_chunk, 0)

    def count(pred):
        n_acc = 4

        def body(c, accs):
            accs = list(accs)
            for i in range(ck // SUBLANES):
                rows = pl.ds(pl.multiple_of(c * ck + i * SUBLANES, SUBLANES), SUBLANES)
                accs[i % n_acc] = accs[i % n_acc] + pred(sc_ref[rows, :]).astype(F32)
            return tuple(accs)

        zero = jnp.zeros((SUBLANES, tq), F32)
        accs = lax.fori_loop(0, n_chunks, body, (zero,) * n_acc)
        return jnp.sum((accs[0] + accs[1]) + (accs[2] + accs[3]), axis=0, keepdims=True)

    u, cnt_u = _kth_largest_key(lambda cand: count(lambda s: s >= cand), (1, tq), topk)
    small = t < topk
    thr = jnp.where(small, F32_LOWEST, _key_to_float(u))
    has_ties = jnp.max(jnp.where(small, 0.0, cnt_u - topk)) > 0.0

    @pl.when(has_ties)
    def _():
        need = topk - count(lambda s: s > thr)

        def select_chunk(c, carry):
            bias, carry = _select_chunk_t(sc_ref[chunk(c), :], thr, need, carry, tril_ref[...])
            sc_ref[chunk(c), :] = bias
            return carry

        lax.fori_loop(0, n_chunks, select_chunk, jnp.zeros((1, tq), F32))

    @pl.when(jnp.logical_not(has_ties))
    def _():
        def select_chunk(c, carry):
            sc_ref[chunk(c), :] = _threshold_bias(sc_ref[chunk(c), :], thr)
            return carry

        lax.fori_loop(0, n_chunks, select_chunk, 0)

    q = q_ref[0]
    ones = jnp.ones((BF16_SUBLANES, ck), BF16)
    q2s = [jnp.concatenate([q[:, (g * Q_PER_KV + i) * HEAD_DIM:(g * Q_PER_KV + i + 1) * HEAD_DIM]
                            for i in range(Q_PER_KV)], axis=0) for g in range(N_KV_HEADS)]

    def key_norm_chunk(c, acc):
        return jnp.maximum(acc, jnp.max(kn_ref[0, chunk(c), :], axis=0, keepdims=True))

    kn_max = lax.fori_loop(0, n_chunks, key_norm_chunk, jnp.zeros((1, LANES), F32))
    first_lane = lax.broadcasted_iota(jnp.int32, (Q_PER_KV * tq, HEAD_DIM), 1) == 0
    ones_d = jnp.ones((HEAD_DIM, HEAD_DIM), BF16)
    q2_shifted = []
    for g in range(N_KV_HEADS):
        qf = q2s[g].astype(F32)
        q_norm2 = _dot((qf * qf).astype(BF16), ones_d)[:, :1]
        bound = jnp.sqrt(q_norm2 * kn_max[:, g:g + 1])
        bound = (bound * BOUND_SLACK + BOUND_FLOOR) * BF16_ROUND_UP
        q2_shifted.append(jnp.concatenate([q2s[g], jnp.where(first_lane, -bound, 0.0).astype(BF16)], axis=1))
    ones_k = jnp.ones((ck, HEAD_DIM), BF16)
    acc_ref[...] = jnp.zeros(acc_ref.shape, F32)

    def attend_shifted(c, carry):
        bias = jnp.concatenate([sc_ref[chunk(c), :]] * Q_PER_KV, axis=1)
        for g in range(N_KV_HEADS):
            gs = slice(g * HEAD_DIM, (g + 1) * HEAD_DIM)
            k1 = jnp.concatenate([kb_ref[0, chunk(c), gs], ones_k], axis=1)
            p = jnp.exp2(_dot_nt(k1, q2_shifted[g]) + bias).astype(BF16)
            v1 = jnp.concatenate([vt_ref[0, gs, chunk(c)], ones], axis=0)
            acc_ref[g] = acc_ref[g] + _dot(v1, p)
        return carry

    lax.fori_loop(0, n_chunks, attend_shifted, 0)
    denom_min = acc_ref[0][HEAD_DIM:HEAD_DIM + 1]
    for g in range(1, N_KV_HEADS):
        denom_min = jnp.minimum(denom_min, acc_ref[g][HEAD_DIM:HEAD_DIM + 1])
    underflow = jnp.logical_not(jnp.min(denom_min) > DENOM_FLOOR)

    @pl.when(underflow)
    def _():
        _attend_running_max(q2s, kb_ref, vt_ref, sc_ref, m_ref, acc_ref, n_chunks, chunk, ones)

    for g in range(N_KV_HEADS):
        acc = acc_ref[g]
        o_t = acc[:HEAD_DIM] / acc[HEAD_DIM:HEAD_DIM + 1]
        for i in range(Q_PER_KV):
            h = g * Q_PER_KV + i
            o_ref[0, :, h * HEAD_DIM:(h + 1) * HEAD_DIM] = o_t[:, i * tq:(i + 1) * tq].T.astype(BF16)


def _attend_running_max(q2s, kb_ref, vt_ref, sc_ref, m_ref, acc_ref, n_chunks, chunk, ones):
    m_ref[...] = jnp.full(m_ref.shape, M_INIT, F32)
    acc_ref[...] = jnp.zeros(acc_ref.shape, F32)

    def attend_chunk(c, carry):
        bias = jnp.concatenate([sc_ref[chunk(c), :]] * Q_PER_KV, axis=1)
        for g in range(N_KV_HEADS):
            gs = slice(g * HEAD_DIM, (g + 1) * HEAD_DIM)
            s = _dot_nt(kb_ref[0, chunk(c), gs], q2s[g]) + bias
            m_old = m_ref[g][:1]
            m_new = jnp.maximum(m_old, jnp.max(s, axis=0, keepdims=True))
            p = jnp.exp2(s - m_new).astype(BF16)
            v1 = jnp.concatenate([vt_ref[0, gs, chunk(c)], ones], axis=0)
            acc_ref[g] = jnp.exp2(m_old - m_new) * acc_ref[g] + _dot(v1, p)
            m_ref[g] = jnp.broadcast_to(m_new, m_ref.shape[1:])
        return carry

    lax.fori_loop(0, n_chunks, attend_chunk, 0)


def _prompt_attn(q, qi, kw, kb, kn, vt, tril):
    b, s, _ = q.shape
    tq = Q_TILE
    assert s % KEY_CHUNK == 0 and s % tq == 0 and tq % LANES == 0
    topk = min(TOPK_MAX, s // 4)
    blk = lambda n: pl.BlockSpec((1, tq, n), lambda i, j: (i, j, 0))
    seq = lambda n: pl.BlockSpec((1, s, n), lambda i, j: (i, 0, 0))
    return pl.pallas_call(
        functools.partial(_prompt_attn_kernel, topk=topk),
        grid=(b, s // tq),
        in_specs=[blk(D_MODEL), blk(N_IDX_HEADS * IDX_DIM), blk(LANES), seq(KV_DIM), seq(LANES),
                  pl.BlockSpec((1, KV_DIM, s), lambda i, j: (i, 0, 0)), seq(LANES), _resident(tril.shape)],
        out_specs=blk(D_MODEL),
        out_shape=jax.ShapeDtypeStruct((b, s, D_MODEL), BF16),
        scratch_shapes=[pltpu.VMEM((s, tq), F32),
                        pltpu.VMEM((N_KV_HEADS, SUBLANES, Q_PER_KV * tq), F32),
                        pltpu.VMEM((N_KV_HEADS, HEAD_DIM + BF16_SUBLANES, Q_PER_KV * tq), F32)],
        compiler_params=pltpu.CompilerParams(dimension_semantics=("arbitrary", "arbitrary"),
                                             vmem_limit_bytes=VMEM_LIMIT_BYTES),
        name="prompt_attn",
    )(q, qi, kw, kb, kn, vt, kw, tril)


def _sample_score_kernel(pt_ref, qi_ref, w_ref, knew_ref, *refs, n_pages, t_new):
    del pt_ref
    page_refs = refs[:n_pages]
    sp_ref, sn_ref = refs[n_pages:]
    qi = qi_ref[0]
    w = (w_ref[0] * (N_IDX_HEADS ** -0.5)) * (IDX_DIM ** -0.5)

    def head_sum(dots):
        r = jnp.maximum(dots, 0.0) * w
        return [jnp.sum(r[i * N_IDX_HEADS:(i + 1) * N_IDX_HEADS], axis=0, keepdims=True)
                for i in range(t_new)]

    group = 4
    for p0 in range(0, n_pages, group):
        keys_t = jnp.concatenate([page_refs[p][0] for p in range(p0, p0 + group)], axis=1).astype(BF16)
        for i, r in enumerate(head_sum(_dot(qi, keys_t))):
            sp_ref[0, i:i + 1, p0 * PAGE_SIZE:(p0 + group) * PAGE_SIZE] = r

    knew = jnp.concatenate([knew_ref[0][:, :IDX_DIM],
                            jnp.zeros((LANES - knew_ref.shape[1], IDX_DIM), F32)], axis=0).astype(BF16)
    kpos = lax.broadcasted_iota(jnp.int32, (1, LANES), 1)
    for i, r in enumerate(head_sum(_dot_nt(qi, knew))):
        sn_ref[0, i:i + 1, :] = jnp.where(kpos <= i, r, NEG_INF)


def _sample_scores(page_table, qi32, w32, knew, cache_kidx_t):
    db, n_pages = page_table.shape
    t_new = qi32.shape[1] // N_IDX_HEADS
    assert n_pages % 4 == 0
    page_spec = lambda p: pl.BlockSpec((1, IDX_DIM, PAGE_SIZE), lambda b, pt: (pt[b, p], 0, 0))
    grid_spec = pltpu.PrefetchScalarGridSpec(
        num_scalar_prefetch=1,
        grid=(db,),
        in_specs=[pl.BlockSpec((1,) + qi32.shape[1:], lambda b, pt: (b, 0, 0)),
                  pl.BlockSpec((1,) + w32.shape[1:], lambda b, pt: (b, 0, 0)),
                  pl.BlockSpec((1,) + knew.shape[1:], lambda b, pt: (b, 0, 0))]
                 + [page_spec(p) for p in range(n_pages)],
        out_specs=[pl.BlockSpec((1, t_new, n_pages * PAGE_SIZE), lambda b, pt: (b, 0, 0)),
                   pl.BlockSpec((1, t_new, LANES), lambda b, pt: (b, 0, 0))],
    )
    return pl.pallas_call(
        functools.partial(_sample_score_kernel, n_pages=n_pages, t_new=t_new),
        grid_spec=grid_spec,
        out_shape=[jax.ShapeDtypeStruct((db, t_new, n_pages * PAGE_SIZE), F32),
                   jax.ShapeDtypeStruct((db, t_new, LANES), F32)],
        compiler_params=pltpu.CompilerParams(dimension_semantics=("arbitrary",),
                                             vmem_limit_bytes=VMEM_LIMIT_BYTES),
        name="sample_scores",
    )(page_table, qi32, w32, knew, *([cache_kidx_t] * n_pages))


def _sample_select_kernel(sp_ref, sn_ref, tri_ref, slot_ref, bn_ref, npast_ref, *, topk):
    rows, past = sp_ref.shape
    ck = KEY_CHUNK
    n_chunks = past // ck

    def count(pred):
        def body(c, cnt):
            s = sp_ref[:, pl.ds(pl.multiple_of(c * ck, ck), ck)]
            return cnt + _lane_partial(pred(s).astype(F32))
        part = lax.fori_loop(0, n_chunks, body, pred(sn_ref[...]).astype(F32))
        return jnp.sum(part, axis=-1, keepdims=True)

    u, _ = _kth_largest_key(lambda cand: count(lambda s: s >= cand), (rows, 1), topk)
    thr = _key_to_float(u)
    need = topk - count(lambda s: s > thr)

    def select_chunk(c, carry):
        ties, taken = carry
        cs = pl.ds(pl.multiple_of(c * ck, ck), ck)
        bias, ties = _select_chunk(sp_ref[:, cs], thr, need, ties, tri_ref[...])
        sel = (bias == 0.0).astype(F32)
        inc = _dot(sel.astype(BF16), tri_ref[...])
        slot_ref[:, cs] = jnp.where(sel > 0.0, taken + inc - 1.0, -1.0).astype(jnp.int32)
        return ties, taken + inc[:, ck - 1:]

    zero = jnp.zeros((rows, 1), F32)
    ties, taken = lax.fori_loop(0, n_chunks, select_chunk, (zero, zero))
    bias, _ = _select_chunk(sn_ref[...], thr, need, ties, tri_ref[:LANES, :LANES])
    bn_ref[...] = bias
    npast_ref[...] = jnp.broadcast_to(taken, npast_ref.shape)


def _sample_select(sp, sn, tri, *, topk):
    n, past = sp.shape
    rows = min(SAMPLE_ROWS, n)
    assert n % rows == 0 and past % KEY_CHUNK == 0
    lane_spec = pl.BlockSpec((rows, LANES), lambda i: (i, 0))
    return pl.pallas_call(
        functools.partial(_sample_select_kernel, topk=topk),
        grid=(n // rows,),
        in_specs=[pl.BlockSpec((rows, past), lambda i: (i, 0)), lane_spec, _resident(tri.shape)],
        out_specs=[pl.BlockSpec((rows, past), lambda i: (i, 0)), lane_spec, lane_spec],
        out_shape=[jax.ShapeDtypeStruct((n, past), jnp.int32), jax.ShapeDtypeStruct((n, LANES), F32),
                   jax.ShapeDtypeStruct((n, LANES), F32)],
        compiler_params=pltpu.CompilerParams(dimension_semantics=("arbitrary",),
                                             vmem_limit_bytes=VMEM_LIMIT_BYTES),
        name="sample_select",
    )(sp, sn, tri)


def _sample_gather(slot, page_table, table_k, table_v, *, topk):
    nq, n_pages, _ = slot.shape
    db = page_table.shape[0]
    t_new = nq // db
    workers = SC_CORES * SC_SUBCORES
    assert nq % workers == 0 and topk % SC_GATHER_WINDOW == 0 and PAGE_SIZE % SC_LANES == 0
    assert t_new & (t_new - 1) == 0
    per_worker = nq // workers
    row_shape = (N_KV_HEADS, HEAD_DIM)
    out = jax.ShapeDtypeStruct((nq, topk) + row_shape, F32)
    mesh = plsc.VectorSubcoreMesh(core_axis_name="c", subcore_axis_name="s")

    @functools.partial(
        pl.kernel, mesh=mesh, out_type=(out, out),
        scratch_types=[pltpu.VMEM((n_pages, PAGE_SIZE), jnp.int32),
                       pltpu.VMEM((n_pages,), jnp.int32),
                       pltpu.VMEM((topk,), jnp.int32),
                       pltpu.VMEM((SC_GATHER_WINDOW,) + row_shape, F32)],
        compiler_params=pltpu.CompilerParams(needs_layout_passes=False),
        name="sample_gather")
    def gather(slot_hbm, pt_hbm, tk_hbm, tv_hbm, kg_hbm, vg_hbm, slot_v, pt_v, idx_v, rows_v):
        worker = lax.axis_index("c") * SC_SUBCORES + lax.axis_index("s")
        lane = lax.iota(jnp.int32, SC_LANES)

        @pl.loop(0, per_worker)
        def _(i):
            q = worker * per_worker + i
            pltpu.sync_copy(slot_hbm.at[q], slot_v)
            pltpu.sync_copy(pt_hbm.at[lax.shift_right_logical(q, int(math.log2(t_new)))], pt_v)

            @pl.loop(0, topk // SC_LANES)
            def _(j):
                idx_v[pl.ds(j * SC_LANES, SC_LANES)] = jnp.zeros((SC_LANES,), jnp.int32)

            @pl.loop(0, n_pages)
            def _(p):
                base = plsc.load_gather(pt_v, [jnp.full((SC_LANES,), p, jnp.int32)]) * PAGE_SIZE
                for l in range(PAGE_SIZE // SC_LANES):
                    s = slot_v[p, pl.ds(l * SC_LANES, SC_LANES)]
                    plsc.store_scatter(idx_v, [s], base + (l * SC_LANES + lane), mask=s >= 0)

            for w in range(topk // SC_GATHER_WINDOW):
                win = pl.ds(w * SC_GATHER_WINDOW, SC_GATHER_WINDOW)
                for table, dst in ((tk_hbm, kg_hbm), (tv_hbm, vg_hbm)):
                    pltpu.sync_copy(table.at[idx_v.at[win]], rows_v)
                    pltpu.sync_copy(rows_v, dst.at[q, win])

    return gather(slot, page_table, table_k, table_v)


def _sample_attn_kernel(q_ref, kg_ref, vg_ref, knew_ref, vnew_ref, bn_ref, npast_ref, o_ref, *, t_new, topk):
    rows = t_new * Q_PER_KV
    n_keys = t_new * topk
    row_query = lax.shift_right_logical(lax.broadcasted_iota(jnp.int32, (rows, 1), 0), int(math.log2(Q_PER_KV)))
    col = lax.broadcasted_iota(jnp.int32, (1, n_keys), 1)
    col_query = lax.shift_right_logical(col, int(math.log2(topk)))
    col_slot = (col & (topk - 1)).astype(F32)
    own = (col_query == row_query) & (col_slot < npast_ref[0][:, :1])
    bias_past = jnp.where(own, 0.0, NEG_INF)
    bias_new = bn_ref[0]
    for h in range(N_KV_HEADS):
        hs = slice(h * HEAD_DIM, (h + 1) * HEAD_DIM)
        q = q_ref[0, h]
        keys = kg_ref[pl.ds(h, n_keys, stride=N_KV_HEADS), :].astype(BF16)
        vals = vg_ref[pl.ds(h, n_keys, stride=N_KV_HEADS), :].astype(BF16)
        s_past = _dot_nt(q, keys) + bias_past
        s_new = _dot_nt(q, knew_ref[0][:, hs]) + bias_new
        m = jnp.maximum(jnp.max(s_past, axis=-1, keepdims=True), jnp.max(s_new, axis=-1, keepdims=True))
        p_past = jnp.exp(s_past - m)
        p_new = jnp.exp(s_new - m)
        denom = jnp.sum(p_past, axis=-1, keepdims=True) + jnp.sum(p_new, axis=-1, keepdims=True)
        o = _dot(p_past.astype(BF16), vals) + _dot(p_new.astype(BF16), vnew_ref[0][:, hs])
        o_ref[0, h] = (o / denom).astype(BF16)


def _sample_attn(q8, kg, vg, knew, vnew, bias_new8, npast8, *, topk):
    db, _, rows, _ = q8.shape
    t_new = rows // Q_PER_KV
    assert topk & (topk - 1) == 0
    gathered = pl.BlockSpec((t_new * topk * N_KV_HEADS, HEAD_DIM), lambda b: (b, 0))
    per_seq = lambda arr: pl.BlockSpec((1,) + arr.shape[1:], lambda b: (b,) + (0,) * (arr.ndim - 1))
    return pl.pallas_call(
        functools.partial(_sample_attn_kernel, t_new=t_new, topk=topk),
        grid=(db,),
        in_specs=[per_seq(q8), gathered, gathered, per_seq(knew), per_seq(vnew), per_seq(bias_new8),
                  per_seq(npast8)],
        out_specs=per_seq(q8),
        out_shape=jax.ShapeDtypeStruct(q8.shape, BF16),
        compiler_params=pltpu.CompilerParams(dimension_semantics=("arbitrary",),
                                             vmem_limit_bytes=VMEM_LIMIT_BYTES),
        name="sample_attn",
    )(q8, kg, vg, knew, vnew, bias_new8, npast8)


def _sample_mixer_b(page_table, q, qi, kw, kb, vb, cache_k, cache_v, cache_kidx, tri):
    db, n_pages = page_table.shape
    n_pool = cache_k.shape[0]
    t_new = q.shape[0] // db
    past = n_pages * PAGE_SIZE
    topk = min(TOPK_MAX, (past + t_new) // 4)
    rows = t_new * N_IDX_HEADS
    qi32 = qi.reshape(db, rows, IDX_DIM)
    w32 = kw[:, IDX_DIM:IDX_DIM + N_IDX_HEADS].reshape(db, rows, 1)
    knew_idx = jnp.pad(kw.reshape(db, t_new, LANES), ((0, 0), (0, SUBLANES - t_new), (0, 0)))
    sp, sn = _sample_scores(page_table, qi32, w32, knew_idx, jnp.swapaxes(cache_kidx, 1, 2))
    slot, bias_n, npast = _sample_select(sp.reshape(db * t_new, past), sn.reshape(db * t_new, LANES), tri,
                                         topk=topk)
    kg, vg = _sample_gather(slot.reshape(db * t_new, n_pages, PAGE_SIZE), page_table,
                            cache_k.reshape(n_pool * PAGE_SIZE, N_KV_HEADS, HEAD_DIM),
                            cache_v.reshape(n_pool * PAGE_SIZE, N_KV_HEADS, HEAD_DIM), topk=topk)
    q8 = q.reshape(db, t_new, N_KV_HEADS, Q_PER_KV, HEAD_DIM).transpose(0, 2, 1, 3, 4)
    q8 = q8.reshape(db, N_KV_HEADS, t_new * Q_PER_KV, HEAD_DIM)
    per_row = lambda arr: jnp.repeat(arr.reshape(db, t_new, LANES), Q_PER_KV, axis=1)
    pad_new = lambda arr: jnp.pad(arr.reshape(db, t_new, KV_DIM), ((0, 0), (0, NEW_ROWS - t_new), (0, 0)))
    flat = lambda arr: arr.reshape(db * t_new * topk * N_KV_HEADS, HEAD_DIM)
    o8 = _sample_attn(q8, flat(kg), flat(vg), pad_new(kb), pad_new(vb), per_row(bias_n), per_row(npast),
                      topk=topk)
    o = o8.reshape(db, N_KV_HEADS, t_new, Q_PER_KV, HEAD_DIM).transpose(0, 2, 1, 3, 4)
    return o.reshape(db * t_new, D_MODEL)


def _prepare_weights(layer, w_in, ln_v_g, ln_v_b, w_s, b_s, w_pa, w_pb, w_o, ln1_g, ln1_b, w_gate, w_up,
                     w_down, ln2_g, ln2_b, t_new):
    w = w_in[layer]
    o_q = 2 * D_MODEL
    o_kv = o_q + D_MODEL
    o_qi = o_kv + 2 * KV_DIM
    o_ki = o_qi + N_IDX_HEADS * IDX_DIM
    o_g = o_ki + IDX_DIM + N_IDX_HEADS
    row = lambda v: v[layer].reshape(1, -1)
    reps = CHUNK // t_new
    shared = dict(
        wuv=w[:, :o_q].astype(BF16), wq=w[:, o_q:o_kv].astype(BF16), wkv=w[:, o_kv:o_qi].astype(BF16),
        wqi=w[:, o_qi:o_ki].astype(BF16),
        wkw=jnp.pad(w[:, o_ki:o_g], ((0, 0), (0, LANES - (o_g - o_ki)))).astype(BF16),
        wg=w[:, o_g:].astype(BF16),
        ln_v_g=row(ln_v_g), ln_v_b=row(ln_v_b), wpa=w_pa[layer].astype(BF16),
        wpb=w_pb[layer].astype(BF16), wo=w_o[layer].astype(BF16), ln1_g=row(ln1_g), ln1_b=row(ln1_b),
        wgate=w_gate[layer].astype(BF16), wup=w_up[layer].astype(BF16), wdown=w_down[layer].astype(BF16),
        ln2_g=row(ln2_g), ln2_b=row(ln2_b))
    prompt = dict(shared, wmix=w_s[layer].astype(BF16),
                  bmix=jnp.repeat(b_s[layer].T, A_GROUP_DIM, axis=1))
    sample = dict(shared, wmix=jnp.tile(w_s[layer][:, :t_new, :t_new], (1, reps, reps)).astype(BF16),
                  bmix=jnp.repeat(jnp.tile(b_s[layer][:, :t_new].T, (reps, 1)), A_GROUP_DIM, axis=1))
    return prompt, sample


def kernel(x_prompt, x_sample, cache_k, cache_v, cache_kidx, page_table, w_in, ln_v_g, ln_v_b, w_s, b_s,
           w_pa, w_pb, w_o, ln1_g, ln1_b, w_gate, w_up, w_down, ln2_g, ln2_b):
    depth = w_in.shape[0]
    alpha = (2 * depth) ** 0.25
    b, s, _ = x_prompt.shape
    db, t_new, _ = x_sample.shape
    n_pages = page_table.shape[1]
    n_pool = cache_k.shape[1]
    past = n_pages * PAGE_SIZE
    assert t_new <= CHUNK and CHUNK % t_new == 0 and t_new <= NEW_ROWS
    tri = jnp.triu(jnp.ones((KEY_CHUNK, KEY_CHUNK), BF16))
    tril = jnp.tril(jnp.ones((KEY_CHUNK, KEY_CHUNK), BF16))

    y_p = x_prompt.reshape(b * s, D_MODEL)
    y_s = x_sample.reshape(db * t_new, D_MODEL)
    outs = [[] for _ in range(7)]
    for layer in range(depth):
        wp, ws = _prepare_weights(layer, w_in, ln_v_g, ln_v_b, w_s, b_s, w_pa, w_pb, w_o, ln1_g, ln1_b,
                                  w_gate, w_up, w_down, ln2_g, ln2_b, t_new)

        a, sgb, q, k_p, v_p, kb, vt, qi, kw, kn = _input_proj(y_p, wp, chunk_len=CHUNK, emit_vnorm=False,
                                                              seq_len=s)
        seq = lambda arr: arr.reshape(b, s, arr.shape[-1])
        bo = _prompt_attn(seq(q), seq(qi), seq(kw), seq(kb), seq(kn), vt, tril)
        y_p = _merge_ffn(y_p, a, sgb, bo.reshape(b * s, D_MODEL), wp, alpha=alpha, name="merge_ffn_prompt")
        outs[0].append(k_p.reshape(b, s, N_KV_HEADS, HEAD_DIM))
        outs[1].append(v_p.reshape(b, s, N_KV_HEADS, HEAD_DIM))
        outs[2].append(kw[:, :IDX_DIM].reshape(b, s, IDX_DIM))

        a, sgb, q, k_s, v_s, kb, vb, qi, kw, _, vn = _input_proj(y_s, ws, chunk_len=t_new, emit_vnorm=True)
        bo = _sample_mixer_b(page_table, q, qi, kw, kb, vb, cache_k[layer], cache_v[layer], cache_kidx[layer],
                             tri)
        y_s = _merge_ffn(y_s, a, sgb, bo, ws, alpha=alpha, name="merge_ffn_sample")
        outs[3].append(k_s.reshape(db, t_new, N_KV_HEADS, HEAD_DIM))
        outs[4].append(v_s.reshape(db, t_new, N_KV_HEADS, HEAD_DIM))
        outs[5].append(kw[:, :IDX_DIM].reshape(db, t_new, IDX_DIM))
        outs[6].append(vn.reshape(db, t_new, D_MODEL))

    return (y_p.reshape(b, s, D_MODEL), y_s.reshape(db, t_new, D_MODEL), *[jnp.stack(o) for o in outs])
```

```python
import functools
import math

import jax
import jax.numpy as jnp
from jax import lax
from jax.experimental import pallas as pl
from jax.experimental.pallas import tpu as pltpu
from jax.experimental.pallas import tpu_sc as plsc

D_MODEL = 1024
CHUNK = 128
A_GROUPS = 8
A_GROUP_DIM = D_MODEL // A_GROUPS
N_HEADS = 8
N_KV_HEADS = 4
Q_PER_KV = N_HEADS // N_KV_HEADS
HEAD_DIM = D_MODEL // N_HEADS
KV_DIM = N_KV_HEADS * HEAD_DIM
N_IDX_HEADS = 8
IDX_DIM = 64
TOPK_MAX = 256
PAGE_SIZE = 128
LN_EPS = 1e-5

LANES = 128
SUBLANES = 8
BF16_SUBLANES = 16
VMEM_LIMIT_BYTES = 56 * 1024 * 1024

TOKEN_TILE = 512
Q_TILE = 512
KEY_CHUNK = 512
SAMPLE_ROWS = 64
SC_CORES = 2
SC_SUBCORES = 16
SC_LANES = 16
SC_GATHER_WINDOW = 64
NEW_ROWS = 128

F32 = jnp.float32
BF16 = jnp.bfloat16
NEG_INF = float("-inf")
F32_LOWEST = float(jnp.finfo(jnp.float32).min)
M_INIT = -1e30
LOG2_E = math.log2(math.e)
BOUND_SLACK = 1.01
BOUND_FLOOR = 1e-6
BF16_ROUND_UP = 1.0 + 2.0 ** -7
DENOM_FLOOR = 1e-30
INT32_MIN = -(2 ** 31)
KEY_HALF_ULP = 2 ** 15


def _dot(a, b):
    return jnp.dot(a, b, preferred_element_type=F32)


def _dot_nt(a, b):
    return lax.dot_general(a, b, (((1,), (1,)), ((), ())), preferred_element_type=F32)


def _gelu(x):
    c = math.sqrt(2.0 / math.pi)
    return x * (0.5 * (1.0 + jnp.tanh(c * (x + 0.044715 * (x * x * x)))))


def _sigmoid(x):
    return 1.0 / (1.0 + jnp.exp(-x))


def _layer_norm(x, g, b):
    mu = jnp.mean(x, axis=-1, keepdims=True)
    xc = x - mu
    var = jnp.mean(xc * xc, axis=-1, keepdims=True)
    return xc * lax.rsqrt(var + LN_EPS) * g + b


def _key_to_float(u):
    c = u ^ jnp.int32(INT32_MIN)
    bits = c ^ ((c >> 31) & jnp.int32(0x7FFFFFFF))
    return lax.bitcast_convert_type(bits, F32)


def _resident(shape):
    nd = len(shape)
    return pl.BlockSpec(shape, lambda *_: (0,) * nd, pipeline_mode=pl.Buffered(1))


def _input_proj_kernel(x_ref, wuv_ref, wq_ref, wkv_ref, wqi_ref, wkw_ref, wg_ref, lng_ref, lnb_ref,
                       wmix_ref, bmix_ref, wpa_ref,
                       a_ref, sgb_ref, q_ref, k_ref, v_ref, kb_ref, vb_ref, qi_ref, kw_ref, kn_ref, *rest,
                       chunk_len, emit_vnorm, transpose_v, q_scale):
    if emit_vnorm:
        vn_ref, aout_ref = rest
    else:
        (aout_ref,) = rest
    tm = x_ref.shape[0]
    xb = x_ref[...].astype(BF16)

    zuv = _dot(xb, wuv_ref[...])
    u = _gelu(zuv[:, :D_MODEL])
    vn = _layer_norm(_gelu(zuv[:, D_MODEL:]), lng_ref[...], lnb_ref[...])
    if emit_vnorm:
        vn_ref[...] = vn
    vnb = vn.astype(BF16)

    row = lax.broadcasted_iota(jnp.int32, (CHUNK, CHUNK), 0)
    col = lax.broadcasted_iota(jnp.int32, (CHUNK, CHUNK), 1)
    same_chunk = (row & ~(chunk_len - 1)) == (col & ~(chunk_len - 1))
    mix_mask = (col <= row) & same_chunk
    for g in range(A_GROUPS):
        gs = slice(g * A_GROUP_DIM, (g + 1) * A_GROUP_DIM)
        wm = jnp.where(mix_mask, wmix_ref[g], jnp.zeros((), BF16))
        for c in range(tm // CHUNK):
            rs = slice(c * CHUNK, (c + 1) * CHUNK)
            s = _dot(wm, vnb[rs, gs]) + bmix_ref[:, gs]
            aout_ref[rs, gs] = (u[rs, gs] * s).astype(BF16)

    pa = _dot(aout_ref[...], wpa_ref[...])
    gates = _dot(xb, wg_ref[...])
    a_ref[...] = (_sigmoid(gates[:, :D_MODEL]) * pa).astype(BF16)
    sgb_ref[...] = _sigmoid(gates[:, D_MODEL:]).astype(BF16)

    q_ref[...] = (_dot(xb, wq_ref[...]) * q_scale).astype(BF16)
    kv = _dot(xb, wkv_ref[...])
    for h in range(N_KV_HEADS):
        k_ref[pl.ds(h, tm, stride=N_KV_HEADS), :] = kv[:, h * HEAD_DIM:(h + 1) * HEAD_DIM]
        v_ref[pl.ds(h, tm, stride=N_KV_HEADS), :] = kv[:, KV_DIM + h * HEAD_DIM:KV_DIM + (h + 1) * HEAD_DIM]
    kb = kv[:, :KV_DIM].astype(BF16)
    kb_ref[...] = kb
    ksq = kb.astype(F32)
    ksq = ksq * ksq
    lane = lax.broadcasted_iota(jnp.int32, (tm, LANES), 1)
    kn = jnp.zeros((tm, LANES), F32)
    for h in range(N_KV_HEADS):
        kn = jnp.where(lane == h, jnp.sum(ksq[:, h * HEAD_DIM:(h + 1) * HEAD_DIM], axis=-1, keepdims=True), kn)
    kn_ref[...] = kn
    if transpose_v:
        vb_ref[0] = kv[:, KV_DIM:].T.astype(BF16)
    else:
        vb_ref[...] = kv[:, KV_DIM:].astype(BF16)
    qi_ref[...] = _dot(xb, wqi_ref[...]).astype(BF16)
    kw_ref[...] = _dot(xb, wkw_ref[...])


def _input_proj(x, wts, *, chunk_len, emit_vnorm, seq_len=None):
    t = x.shape[0]
    tm = min(TOKEN_TILE, t)
    assert t % tm == 0 and tm % CHUNK == 0
    row_spec = lambda n, rows_per_token=1: pl.BlockSpec((tm * rows_per_token, n), lambda i: (i, 0))
    out_shapes = [
        jax.ShapeDtypeStruct((t, D_MODEL), BF16),
        jax.ShapeDtypeStruct((t, D_MODEL), BF16),
        jax.ShapeDtypeStruct((t, D_MODEL), BF16),
        jax.ShapeDtypeStruct((t * N_KV_HEADS, HEAD_DIM), F32),
        jax.ShapeDtypeStruct((t * N_KV_HEADS, HEAD_DIM), F32),
        jax.ShapeDtypeStruct((t, KV_DIM), BF16),
        jax.ShapeDtypeStruct((t, KV_DIM), BF16),
        jax.ShapeDtypeStruct((t, N_IDX_HEADS * IDX_DIM), BF16),
        jax.ShapeDtypeStruct((t, LANES), F32),
        jax.ShapeDtypeStruct((t, LANES), F32),
    ]
    out_specs = [row_spec(s.shape[1], s.shape[0] // t) for s in out_shapes]
    if seq_len is not None:
        assert seq_len % tm == 0 and t % seq_len == 0
        tiles = seq_len // tm
        out_shapes[6] = jax.ShapeDtypeStruct((t // seq_len, KV_DIM, seq_len), BF16)
        out_specs[6] = pl.BlockSpec((1, KV_DIM, tm), lambda i: (i // tiles, 0, i % tiles))
    if emit_vnorm:
        out_shapes.append(jax.ShapeDtypeStruct((t, D_MODEL), F32))
        out_specs.append(row_spec(D_MODEL))
    weights = (wts["wuv"], wts["wq"], wts["wkv"], wts["wqi"], wts["wkw"], wts["wg"], wts["ln_v_g"],
               wts["ln_v_b"], wts["wmix"], wts["bmix"], wts["wpa"])
    return pl.pallas_call(
        functools.partial(_input_proj_kernel, chunk_len=chunk_len, emit_vnorm=emit_vnorm,
                          transpose_v=seq_len is not None,
                          q_scale=HEAD_DIM ** -0.5 * (LOG2_E if seq_len is not None else 1.0)),
        grid=(t // tm,),
        in_specs=[row_spec(D_MODEL)] + [_resident(w.shape) for w in weights],
        out_specs=out_specs,
        out_shape=out_shapes,
        scratch_shapes=[pltpu.VMEM((tm, D_MODEL), BF16)],
        compiler_params=pltpu.CompilerParams(dimension_semantics=("arbitrary",),
                                             vmem_limit_bytes=VMEM_LIMIT_BYTES),
        name="input_proj_sample" if emit_vnorm else "input_proj_prompt",
    )(x, *weights)


def _merge_ffn_kernel(x_ref, a_ref, sgb_ref, bo_ref, wpb_ref, wo_ref, ln1g_ref, ln1b_ref, wgate_ref,
                      wup_ref, wdown_ref, ln2g_ref, ln2b_ref, y_ref, *, alpha):
    pb = _dot(bo_ref[...], wpb_ref[...])
    merged = a_ref[...].astype(F32) + sgb_ref[...].astype(F32) * pb
    mix = _dot(merged.astype(BF16), wo_ref[...])
    x1 = _layer_norm(alpha * x_ref[...] + mix, ln1g_ref[...], ln1b_ref[...])
    x1b = x1.astype(BF16)
    hg = _dot(x1b, wgate_ref[...])
    hu = _dot(x1b, wup_ref[...])
    h = (hg * _sigmoid(hg)) * hu
    f = _dot(h.astype(BF16), wdown_ref[...])
    y_ref[...] = _layer_norm(alpha * x1 + f, ln2g_ref[...], ln2b_ref[...])


def _merge_ffn(x, a, sgb, bo, wts, *, alpha, name):
    t = x.shape[0]
    tm = min(TOKEN_TILE, t)
    assert t % tm == 0
    row_spec = pl.BlockSpec((tm, D_MODEL), lambda i: (i, 0))
    weights = (wts["wpb"], wts["wo"], wts["ln1_g"], wts["ln1_b"], wts["wgate"], wts["wup"], wts["wdown"],
               wts["ln2_g"], wts["ln2_b"])
    return pl.pallas_call(
        functools.partial(_merge_ffn_kernel, alpha=alpha),
        grid=(t // tm,),
        in_specs=[row_spec] * 4 + [_resident(w.shape) for w in weights],
        out_specs=row_spec,
        out_shape=jax.ShapeDtypeStruct((t, D_MODEL), F32),
        compiler_params=pltpu.CompilerParams(dimension_semantics=("arbitrary",),
                                             vmem_limit_bytes=VMEM_LIMIT_BYTES),
        name=name,
    )(x, a, sgb, bo, *weights)


def _lane_partial(x):
    acc = x[:, :LANES]
    for i in range(1, x.shape[1] // LANES):
        acc = acc + x[:, i * LANES:(i + 1) * LANES]
    return acc


def _bisect_key(count_ge_key, shape, topk, *, base=None, hi_bit=31, n_bits=32):
    def bit_body(i, carry):
        d, cnt_d = carry
        cand = d | lax.shift_left(jnp.int32(1), hi_bit - i)
        cnt = count_ge_key(cand if base is None else base + cand)
        take = cnt >= topk
        return jnp.where(take, cand, d), jnp.where(take, cnt, cnt_d)
    d, cnt = lax.fori_loop(0, n_bits, bit_body, (jnp.zeros(shape, jnp.int32), jnp.zeros(shape, F32)))
    return (d if base is None else base + d), cnt


def _kth_largest_key(count_ge, shape, topk):
    return _bisect_key(lambda key: count_ge(_key_to_float(key)), shape, topk)


def _sublane_partial(x):
    parts = [x[i * SUBLANES:(i + 1) * SUBLANES] for i in range(x.shape[0] // SUBLANES)]
    while len(parts) > 1:
        parts = [a + b for a, b in zip(parts[::2], parts[1::2])] + parts[len(parts) - len(parts) % 2:]
    return parts[0]


def _select_chunk_t(s, thr, need, carry, tril):
    gt = s > thr
    eq = s == thr
    eqf = eq.astype(F32)
    inc = _dot(tril, eqf.astype(BF16))
    before = carry + inc - eqf
    sel = gt | (eq & (before < need))
    return jnp.where(sel, 0.0, NEG_INF), carry + inc[inc.shape[0] - 1:]


def _threshold_bias(s, thr):
    return jnp.where(s >= thr, 0.0, NEG_INF)


def _select_chunk(s, thr, need, carry, tri):
    gt = s > thr
    eq = s == thr
    eqf = eq.astype(F32)
    inc = _dot(eqf.astype(BF16), tri)
    before = carry + inc - eqf
    sel = gt | (eq & (before < need))
    bias = jnp.where(sel, 0.0, NEG_INF)
    return bias, carry + inc[:, inc.shape[1] - 1:]


def _prompt_attn_kernel(q_ref, qi_ref, kwq_ref, kb_ref, kn_ref, vt_ref, kw_ref, tril_ref, o_ref,
                        sc_ref, sbf_ref, m_ref, acc_ref, *, topk):
    tq = q_ref.shape[1]
    ck = KEY_CHUNK
    j = pl.program_id(1)
    n_chunks = lax.shift_right_logical((j + 1) * tq + (ck - 1), int(math.log2(ck)))
    t = j * tq + lax.broadcasted_iota(jnp.int32, (1, tq), 1)
    chunk = lambda c: pl.ds(pl.multiple_of(c * ck, ck), ck)

    qi = qi_ref[0]
    w_t = kwq_ref[0].T[IDX_DIM:IDX_DIM + N_IDX_HEADS]
    w_t = (w_t * (N_IDX_HEADS ** -0.5)) * (IDX_DIM ** -0.5)
    qi_pairs = [jnp.concatenate([qi[:, h * IDX_DIM:(h + 1) * IDX_DIM] for h in (2 * p, 2 * p + 1)], axis=0)
                for p in range(N_IDX_HEADS // 2)]
    w_pairs = [jnp.concatenate([w_t[h:h + 1] for h in (2 * p, 2 * p + 1)], axis=1)
               for p in range(N_IDX_HEADS // 2)]

    def score_chunk(c, carry):
        kc = kw_ref[0, chunk(c), :][:, :IDX_DIM].astype(BF16)
        acc = jnp.zeros((ck, tq), F32)
        for p in range(N_IDX_HEADS // 2):
            d = jnp.maximum(_dot_nt(kc, qi_pairs[p]), 0.0) * w_pairs[p]
            acc = acc + (d[:, :tq] + d[:, tq:])
        kpos = c * ck + lax.broadcasted_iota(jnp.int32, (ck, 1), 0)
        masked = jnp.where(kpos <= t, acc, NEG_INF)
        sc_ref[chunk(c), :] = masked
        sbf_ref[chunk(c), :] = masked.astype(BF16)
        return carry

    lax.fori_loop(0, n_chunks, score_chunk, 0)

    def count_rounded(key):
        cand = _key_to_float(key).astype(BF16)
        n_acc = 4

        def body(c, accs):
            accs = list(accs)
            for i in range(ck // BF16_SUBLANES):
                rows = pl.ds(pl.multiple_of(c * ck + i * BF16_SUBLANES, BF16_SUBLANES), BF16_SUBLANES)
                accs[i % n_acc] = accs[i % n_acc] + jnp.where(sbf_ref[rows, :] >= cand, one, zero)
            return tuple(accs)

        one = jnp.ones((BF16_SUBLANES, tq), BF16)
        zero = jnp.zeros((BF16_SUBLANES, tq), BF16)
        accs = lax.fori_loop(0, n_chunks, body, (zero,) * n_acc)
        total = (accs[0].astype(F32) + accs[1].astype(F32)) + (accs[2].astype(F32) + accs[3].astype(F32))
        return jnp.sum(total, axis=0, keepdims=True)

    def count(pred):
        n_acc = 4

        def body(c, accs):
            accs = list(accs)
            for i in range(ck // SUBLANES):
                rows = pl.ds(pl.multiple_of(c * ck + i * SUBLANES, SUBLANES), SUBLANES)
                accs[i % n_acc] = accs[i % n_acc] + pred(sc_ref[rows, :]).astype(F32)
            return tuple(accs)

        zero = jnp.zeros((SUBLANES, tq), F32)
        accs = lax.fori_loop(0, n_chunks, body, (zero,) * n_acc)
        return jnp.sum((accs[0] + accs[1]) + (accs[2] + accs[3]), axis=0, keepdims=True)

    coarse, _ = _bisect_key(count_rounded, (1, tq), topk, n_bits=16)
    u, cnt_u = _bisect_key(lambda key: count(lambda s: s >= _key_to_float(key)), (1, tq), topk,
                           base=coarse - KEY_HALF_ULP, hi_bit=16, n_bits=17)
    small = t < topk
    thr = jnp.where(small, F32_LOWEST, _key_to_float(u))
    excess = jnp.where(cnt_u == 0.0, 1.0, cnt_u - topk)
    has_ties = jnp.max(jnp.where(small, 0.0, excess)) > 0.0

    @pl.when(has_ties)
    def _():
        need = topk - count(lambda s: s > thr)

        def select_chunk(c, carry):
            bias, carry = _select_chunk_t(sc_ref[chunk(c), :], thr, need, carry, tril_ref[...])
            sc_ref[chunk(c), :] = bias
            return carry

        lax.fori_loop(0, n_chunks, select_chunk, jnp.zeros((1, tq), F32))

    @pl.when(jnp.logical_not(has_ties))
    def _():
        def select_chunk(c, carry):
            sc_ref[chunk(c), :] = _threshold_bias(sc_ref[chunk(c), :], thr)
            return carry

        lax.fori_loop(0, n_chunks, select_chunk, 0)

    q = q_ref[0]
    ones = jnp.ones((BF16_SUBLANES, ck), BF16)
    q2s = [jnp.concatenate([q[:, (g * Q_PER_KV + i) * HEAD_DIM:(g * Q_PER_KV + i + 1) * HEAD_DIM]
                            for i in range(Q_PER_KV)], axis=0) for g in range(N_KV_HEADS)]

    def key_norm_chunk(c, acc):
        return jnp.maximum(acc, jnp.max(kn_ref[0, chunk(c), :], axis=0, keepdims=True))

    kn_max = lax.fori_loop(0, n_chunks, key_norm_chunk, jnp.zeros((1, LANES), F32))
    first_lane = lax.broadcasted_iota(jnp.int32, (Q_PER_KV * tq, HEAD_DIM), 1) == 0
    ones_d = jnp.ones((HEAD_DIM, HEAD_DIM), BF16)
    q2_shifted = []
    for g in range(N_KV_HEADS):
        qf = q2s[g].astype(F32)
        q_norm2 = _dot((qf * qf).astype(BF16), ones_d)[:, :1]
        bound = jnp.sqrt(q_norm2 * kn_max[:, g:g + 1])
        bound = (bound * BOUND_SLACK + BOUND_FLOOR) * BF16_ROUND_UP
        q2_shifted.append(jnp.concatenate([q2s[g], jnp.where(first_lane, -bound, 0.0).astype(BF16)], axis=1))
    ones_k = jnp.ones((ck, HEAD_DIM), BF16)
    acc_ref[...] = jnp.zeros(acc_ref.shape, F32)

    def attend_shifted(c, carry):
        bias = jnp.concatenate([sc_ref[chunk(c), :]] * Q_PER_KV, axis=1)
        for g in range(N_KV_HEADS):
            gs = slice(g * HEAD_DIM, (g + 1) * HEAD_DIM)
            k1 = jnp.concatenate([kb_ref[0, chunk(c), gs], ones_k], axis=1)
            p = jnp.exp2(_dot_nt(k1, q2_shifted[g]) + bias).astype(BF16)
            v1 = jnp.concatenate([vt_ref[0, gs, chunk(c)], ones], axis=0)
            acc_ref[g] = acc_ref[g] + _dot(v1, p)
        return carry

    lax.fori_loop(0, n_chunks, attend_shifted, 0)
    denom_min = acc_ref[0][HEAD_DIM:HEAD_DIM + 1]
    for g in range(1, N_KV_HEADS):
        denom_min = jnp.minimum(denom_min, acc_ref[g][HEAD_DIM:HEAD_DIM + 1])
    underflow = jnp.logical_not(jnp.min(denom_min) > DENOM_FLOOR)

    @pl.when(underflow)
    def _():
        _attend_running_max(q2s, kb_ref, vt_ref, sc_ref, m_ref, acc_ref, n_chunks, chunk, ones)

    for g in range(N_KV_HEADS):
        acc = acc_ref[g]
        o_t = acc[:HEAD_DIM] / acc[HEAD_DIM:HEAD_DIM + 1]
        for i in range(Q_PER_KV):
            h = g * Q_PER_KV + i
            o_ref[0, :, h * HEAD_DIM:(h + 1) * HEAD_DIM] = o_t[:, i * tq:(i + 1) * tq].T.astype(BF16)


def _attend_running_max(q2s, kb_ref, vt_ref, sc_ref, m_ref, acc_ref, n_chunks, chunk, ones):
    m_ref[...] = jnp.full(m_ref.shape, M_INIT, F32)
    acc_ref[...] = jnp.zeros(acc_ref.shape, F32)

    def attend_chunk(c, carry):
        bias = jnp.concatenate([sc_ref[chunk(c), :]] * Q_PER_KV, axis=1)
        for g in range(N_KV_HEADS):
            gs = slice(g * HEAD_DIM, (g + 1) * HEAD_DIM)
            s = _dot_nt(kb_ref[0, chunk(c), gs], q2s[g]) + bias
            m_old = m_ref[g][:1]
            m_new = jnp.maximum(m_old, jnp.max(s, axis=0, keepdims=True))
            p = jnp.exp2(s - m_new).astype(BF16)
            v1 = jnp.concatenate([vt_ref[0, gs, chunk(c)], ones], axis=0)
            acc_ref[g] = jnp.exp2(m_old - m_new) * acc_ref[g] + _dot(v1, p)
            m_ref[g] = jnp.broadcast_to(m_new, m_ref.shape[1:])
        return carry

    lax.fori_loop(0, n_chunks, attend_chunk, 0)


def _prompt_attn(q, qi, kw, kb, kn, vt, tril):
    b, s, _ = q.shape
    tq = Q_TILE
    assert s % KEY_CHUNK == 0 and s % tq == 0 and tq % LANES == 0
    topk = min(TOPK_MAX, s // 4)
    blk = lambda n: pl.BlockSpec((1, tq, n), lambda i, j: (i, j, 0))
    seq = lambda n: pl.BlockSpec((1, s, n), lambda i, j: (i, 0, 0))
    return pl.pallas_call(
        functools.partial(_prompt_attn_kernel, topk=topk),
        grid=(b, s // tq),
        in_specs=[blk(D_MODEL), blk(N_IDX_HEADS * IDX_DIM), blk(LANES), seq(KV_DIM), seq(LANES),
                  pl.BlockSpec((1, KV_DIM, s), lambda i, j: (i, 0, 0)), seq(LANES), _resident(tril.shape)],
        out_specs=blk(D_MODEL),
        out_shape=jax.ShapeDtypeStruct((b, s, D_MODEL), BF16),
        scratch_shapes=[pltpu.VMEM((s, tq), F32),
                        pltpu.VMEM((s, tq), BF16),
                        pltpu.VMEM((N_KV_HEADS, SUBLANES, Q_PER_KV * tq), F32),
                        pltpu.VMEM((N_KV_HEADS, HEAD_DIM + BF16_SUBLANES, Q_PER_KV * tq), F32)],
        compiler_params=pltpu.CompilerParams(dimension_semantics=("arbitrary", "arbitrary"),
                                             vmem_limit_bytes=VMEM_LIMIT_BYTES),
        name="prompt_attn",
    )(q, qi, kw, kb, kn, vt, kw, tril)


def _sample_score_kernel(pt_ref, qi_ref, w_ref, knew_ref, *refs, n_pages, t_new):
    del pt_ref
    page_refs = refs[:n_pages]
    sp_ref, sn_ref = refs[n_pages:]
    qi = qi_ref[0]
    w = (w_ref[0] * (N_IDX_HEADS ** -0.5)) * (IDX_DIM ** -0.5)

    def head_sum(dots):
        r = jnp.maximum(dots, 0.0) * w
        return [jnp.sum(r[i * N_IDX_HEADS:(i + 1) * N_IDX_HEADS], axis=0, keepdims=True)
                for i in range(t_new)]

    group = 4
    for p0 in range(0, n_pages, group):
        keys_t = jnp.concatenate([page_refs[p][0] for p in range(p0, p0 + group)], axis=1).astype(BF16)
        for i, r in enumerate(head_sum(_dot(qi, keys_t))):
            sp_ref[0, i:i + 1, p0 * PAGE_SIZE:(p0 + group) * PAGE_SIZE] = r

    knew = jnp.concatenate([knew_ref[0][:, :IDX_DIM],
                            jnp.zeros((LANES - knew_ref.shape[1], IDX_DIM), F32)], axis=0).astype(BF16)
    kpos = lax.broadcasted_iota(jnp.int32, (1, LANES), 1)
    for i, r in enumerate(head_sum(_dot_nt(qi, knew))):
        sn_ref[0, i:i + 1, :] = jnp.where(kpos <= i, r, NEG_INF)


def _sample_scores(page_table, qi32, w32, knew, cache_kidx_t):
    db, n_pages = page_table.shape
    t_new = qi32.shape[1] // N_IDX_HEADS
    assert n_pages % 4 == 0
    page_spec = lambda p: pl.BlockSpec((1, IDX_DIM, PAGE_SIZE), lambda b, pt: (pt[b, p], 0, 0))
    grid_spec = pltpu.PrefetchScalarGridSpec(
        num_scalar_prefetch=1,
        grid=(db,),
        in_specs=[pl.BlockSpec((1,) + qi32.shape[1:], lambda b, pt: (b, 0, 0)),
                  pl.BlockSpec((1,) + w32.shape[1:], lambda b, pt: (b, 0, 0)),
                  pl.BlockSpec((1,) + knew.shape[1:], lambda b, pt: (b, 0, 0))]
                 + [page_spec(p) for p in range(n_pages)],
        out_specs=[pl.BlockSpec((1, t_new, n_pages * PAGE_SIZE), lambda b, pt: (b, 0, 0)),
                   pl.BlockSpec((1, t_new, LANES), lambda b, pt: (b, 0, 0))],
    )
    return pl.pallas_call(
        functools.partial(_sample_score_kernel, n_pages=n_pages, t_new=t_new),
        grid_spec=grid_spec,
        out_shape=[jax.ShapeDtypeStruct((db, t_new, n_pages * PAGE_SIZE), F32),
                   jax.ShapeDtypeStruct((db, t_new, LANES), F32)],
        compiler_params=pltpu.CompilerParams(dimension_semantics=("arbitrary",),
                                             vmem_limit_bytes=VMEM_LIMIT_BYTES),
        name="sample_scores",
    )(page_table, qi32, w32, knew, *([cache_kidx_t] * n_pages))


def _sample_select_kernel(sp_ref, sn_ref, tri_ref, slot_ref, bn_ref, npast_ref, *, topk):
    rows, past = sp_ref.shape
    ck = KEY_CHUNK
    n_chunks = past // ck

    def count(pred):
        def body(c, cnt):
            s = sp_ref[:, pl.ds(pl.multiple_of(c * ck, ck), ck)]
            return cnt + _lane_partial(pred(s).astype(F32))
        part = lax.fori_loop(0, n_chunks, body, pred(sn_ref[...]).astype(F32))
        return jnp.sum(part, axis=-1, keepdims=True)

    u, _ = _kth_largest_key(lambda cand: count(lambda s: s >= cand), (rows, 1), topk)
    thr = _key_to_float(u)
    need = topk - count(lambda s: s > thr)

    def select_chunk(c, carry):
        ties, taken = carry
        cs = pl.ds(pl.multiple_of(c * ck, ck), ck)
        bias, ties = _select_chunk(sp_ref[:, cs], thr, need, ties, tri_ref[...])
        sel = (bias == 0.0).astype(F32)
        inc = _dot(sel.astype(BF16), tri_ref[...])
        slot_ref[:, cs] = jnp.where(sel > 0.0, taken + inc - 1.0, -1.0).astype(jnp.int32)
        return ties, taken + inc[:, ck - 1:]

    zero = jnp.zeros((rows, 1), F32)
    ties, taken = lax.fori_loop(0, n_chunks, select_chunk, (zero, zero))
    bias, _ = _select_chunk(sn_ref[...], thr, need, ties, tri_ref[:LANES, :LANES])
    bn_ref[...] = bias
    npast_ref[...] = jnp.broadcast_to(taken, npast_ref.shape)


def _sample_select(sp, sn, tri, *, topk):
    n, past = sp.shape
    rows = min(SAMPLE_ROWS, n)
    assert n % rows == 0 and past % KEY_CHUNK == 0
    lane_spec = pl.BlockSpec((rows, LANES), lambda i: (i, 0))
    return pl.pallas_call(
        functools.partial(_sample_select_kernel, topk=topk),
        grid=(n // rows,),
        in_specs=[pl.BlockSpec((rows, past), lambda i: (i, 0)), lane_spec, _resident(tri.shape)],
        out_specs=[pl.BlockSpec((rows, past), lambda i: (i, 0)), lane_spec, lane_spec],
        out_shape=[jax.ShapeDtypeStruct((n, past), jnp.int32), jax.ShapeDtypeStruct((n, LANES), F32),
                   jax.ShapeDtypeStruct((n, LANES), F32)],
        compiler_params=pltpu.CompilerParams(dimension_semantics=("arbitrary",),
                                             vmem_limit_bytes=VMEM_LIMIT_BYTES),
        name="sample_select",
    )(sp, sn, tri)


def _sample_gather(slot, page_table, table_k, table_v, *, topk):
    nq, n_pages, _ = slot.shape
    db = page_table.shape[0]
    t_new = nq // db
    workers = SC_CORES * SC_SUBCORES
    assert nq % workers == 0 and topk % SC_GATHER_WINDOW == 0 and PAGE_SIZE % SC_LANES == 0
    assert t_new & (t_new - 1) == 0
    per_worker = nq // workers
    row_shape = (N_KV_HEADS, HEAD_DIM)
    out = jax.ShapeDtypeStruct((nq, topk) + row_shape, F32)
    mesh = plsc.VectorSubcoreMesh(core_axis_name="c", subcore_axis_name="s")

    @functools.partial(
        pl.kernel, mesh=mesh, out_type=(out, out),
        scratch_types=[pltpu.VMEM((n_pages, PAGE_SIZE), jnp.int32),
                       pltpu.VMEM((n_pages,), jnp.int32),
                       pltpu.VMEM((topk,), jnp.int32),
                       pltpu.VMEM((SC_GATHER_WINDOW,) + row_shape, F32)],
        compiler_params=pltpu.CompilerParams(needs_layout_passes=False),
        name="sample_gather")
    def gather(slot_hbm, pt_hbm, tk_hbm, tv_hbm, kg_hbm, vg_hbm, slot_v, pt_v, idx_v, rows_v):
        worker = lax.axis_index("c") * SC_SUBCORES + lax.axis_index("s")
        lane = lax.iota(jnp.int32, SC_LANES)

        @pl.loop(0, per_worker)
        def _(i):
            q = worker * per_worker + i
            pltpu.sync_copy(slot_hbm.at[q], slot_v)
            pltpu.sync_copy(pt_hbm.at[lax.shift_right_logical(q, int(math.log2(t_new)))], pt_v)

            @pl.loop(0, topk // SC_LANES)
            def _(j):
                idx_v[pl.ds(j * SC_LANES, SC_LANES)] = jnp.zeros((SC_LANES,), jnp.int32)

            @pl.loop(0, n_pages)
            def _(p):
                base = plsc.load_gather(pt_v, [jnp.full((SC_LANES,), p, jnp.int32)]) * PAGE_SIZE
                for l in range(PAGE_SIZE // SC_LANES):
                    s = slot_v[p, pl.ds(l * SC_LANES, SC_LANES)]
                    plsc.store_scatter(idx_v, [s], base + (l * SC_LANES + lane), mask=s >= 0)

            for w in range(topk // SC_GATHER_WINDOW):
                win = pl.ds(w * SC_GATHER_WINDOW, SC_GATHER_WINDOW)
                for table, dst in ((tk_hbm, kg_hbm), (tv_hbm, vg_hbm)):
                    pltpu.sync_copy(table.at[idx_v.at[win]], rows_v)
                    pltpu.sync_copy(rows_v, dst.at[q, win])

    return gather(slot, page_table, table_k, table_v)


def _sample_attn_kernel(q_ref, kg_ref, vg_ref, knew_ref, vnew_ref, bn_ref, npast_ref, o_ref, *, t_new, topk):
    rows = t_new * Q_PER_KV
    n_keys = t_new * topk
    row_query = lax.shift_right_logical(lax.broadcasted_iota(jnp.int32, (rows, 1), 0), int(math.log2(Q_PER_KV)))
    col = lax.broadcasted_iota(jnp.int32, (1, n_keys), 1)
    col_query = lax.shift_right_logical(col, int(math.log2(topk)))
    col_slot = (col & (topk - 1)).astype(F32)
    own = (col_query == row_query) & (col_slot < npast_ref[0][:, :1])
    bias_past = jnp.where(own, 0.0, NEG_INF)
    bias_new = bn_ref[0]
    for h in range(N_KV_HEADS):
        hs = slice(h * HEAD_DIM, (h + 1) * HEAD_DIM)
        q = q_ref[0, h]
        keys = kg_ref[pl.ds(h, n_keys, stride=N_KV_HEADS), :].astype(BF16)
        vals = vg_ref[pl.ds(h, n_keys, stride=N_KV_HEADS), :].astype(BF16)
        s_past = _dot_nt(q, keys) + bias_past
        s_new = _dot_nt(q, knew_ref[0][:, hs]) + bias_new
        m = jnp.maximum(jnp.max(s_past, axis=-1, keepdims=True), jnp.max(s_new, axis=-1, keepdims=True))
        p_past = jnp.exp(s_past - m)
        p_new = jnp.exp(s_new - m)
        denom = jnp.sum(p_past, axis=-1, keepdims=True) + jnp.sum(p_new, axis=-1, keepdims=True)
        o = _dot(p_past.astype(BF16), vals) + _dot(p_new.astype(BF16), vnew_ref[0][:, hs])
        o_ref[0, h] = (o / denom).astype(BF16)


def _sample_attn(q8, kg, vg, knew, vnew, bias_new8, npast8, *, topk):
    db, _, rows, _ = q8.shape
    t_new = rows // Q_PER_KV
    assert topk & (topk - 1) == 0
    gathered = pl.BlockSpec((t_new * topk * N_KV_HEADS, HEAD_DIM), lambda b: (b, 0))
    per_seq = lambda arr: pl.BlockSpec((1,) + arr.shape[1:], lambda b: (b,) + (0,) * (arr.ndim - 1))
    return pl.pallas_call(
        functools.partial(_sample_attn_kernel, t_new=t_new, topk=topk),
        grid=(db,),
        in_specs=[per_seq(q8), gathered, gathered, per_seq(knew), per_seq(vnew), per_seq(bias_new8),
                  per_seq(npast8)],
        out_specs=per_seq(q8),
        out_shape=jax.ShapeDtypeStruct(q8.shape, BF16),
        compiler_params=pltpu.CompilerParams(dimension_semantics=("arbitrary",),
                                             vmem_limit_bytes=VMEM_LIMIT_BYTES),
        name="sample_attn",
    )(q8, kg, vg, knew, vnew, bias_new8, npast8)


def _sample_mixer_b(page_table, q, qi, kw, kb, vb, cache_k, cache_v, cache_kidx, tri):
    db, n_pages = page_table.shape
    n_pool = cache_k.shape[0]
    t_new = q.shape[0] // db
    past = n_pages * PAGE_SIZE
    topk = min(TOPK_MAX, (past + t_new) // 4)
    rows = t_new * N_IDX_HEADS
    qi32 = qi.reshape(db, rows, IDX_DIM)
    w32 = kw[:, IDX_DIM:IDX_DIM + N_IDX_HEADS].reshape(db, rows, 1)
    knew_idx = jnp.pad(kw.reshape(db, t_new, LANES), ((0, 0), (0, SUBLANES - t_new), (0, 0)))
    sp, sn = _sample_scores(page_table, qi32, w32, knew_idx, jnp.swapaxes(cache_kidx, 1, 2))
    slot, bias_n, npast = _sample_select(sp.reshape(db * t_new, past), sn.reshape(db * t_new, LANES), tri,
                                         topk=topk)
    kg, vg = _sample_gather(slot.reshape(db * t_new, n_pages, PAGE_SIZE), page_table,
                            cache_k.reshape(n_pool * PAGE_SIZE, N_KV_HEADS, HEAD_DIM),
                            cache_v.reshape(n_pool * PAGE_SIZE, N_KV_HEADS, HEAD_DIM), topk=topk)
    q8 = q.reshape(db, t_new, N_KV_HEADS, Q_PER_KV, HEAD_DIM).transpose(0, 2, 1, 3, 4)
    q8 = q8.reshape(db, N_KV_HEADS, t_new * Q_PER_KV, HEAD_DIM)
    per_row = lambda arr: jnp.repeat(arr.reshape(db, t_new, LANES), Q_PER_KV, axis=1)
    pad_new = lambda arr: jnp.pad(arr.reshape(db, t_new, KV_DIM), ((0, 0), (0, NEW_ROWS - t_new), (0, 0)))
    flat = lambda arr: arr.reshape(db * t_new * topk * N_KV_HEADS, HEAD_DIM)
    o8 = _sample_attn(q8, flat(kg), flat(vg), pad_new(kb), pad_new(vb), per_row(bias_n), per_row(npast),
                      topk=topk)
    o = o8.reshape(db, N_KV_HEADS, t_new, Q_PER_KV, HEAD_DIM).transpose(0, 2, 1, 3, 4)
    return o.reshape(db * t_new, D_MODEL)


def _prepare_weights(layer, w_in, ln_v_g, ln_v_b, w_s, b_s, w_pa, w_pb, w_o, ln1_g, ln1_b, w_gate, w_up,
                     w_down, ln2_g, ln2_b, t_new):
    w = w_in[layer]
    o_q = 2 * D_MODEL
    o_kv = o_q + D_MODEL
    o_qi = o_kv + 2 * KV_DIM
    o_ki = o_qi + N_IDX_HEADS * IDX_DIM
    o_g = o_ki + IDX_DIM + N_IDX_HEADS
    row = lambda v: v[layer].reshape(1, -1)
    reps = CHUNK // t_new
    shared = dict(
        wuv=w[:, :o_q].astype(BF16), wq=w[:, o_q:o_kv].astype(BF16), wkv=w[:, o_kv:o_qi].astype(BF16),
        wqi=w[:, o_qi:o_ki].astype(BF16),
        wkw=jnp.pad(w[:, o_ki:o_g], ((0, 0), (0, LANES - (o_g - o_ki)))).astype(BF16),
        wg=w[:, o_g:].astype(BF16),
        ln_v_g=row(ln_v_g), ln_v_b=row(ln_v_b), wpa=w_pa[layer].astype(BF16),
        wpb=w_pb[layer].astype(BF16), wo=w_o[layer].astype(BF16), ln1_g=row(ln1_g), ln1_b=row(ln1_b),
        wgate=w_gate[layer].astype(BF16), wup=w_up[layer].astype(BF16), wdown=w_down[layer].astype(BF16),
        ln2_g=row(ln2_g), ln2_b=row(ln2_b))
    prompt = dict(shared, wmix=w_s[layer].astype(BF16),
                  bmix=jnp.repeat(b_s[layer].T, A_GROUP_DIM, axis=1))
    sample = dict(shared, wmix=jnp.tile(w_s[layer][:, :t_new, :t_new], (1, reps, reps)).astype(BF16),
                  bmix=jnp.repeat(jnp.tile(b_s[layer][:, :t_new].T, (reps, 1)), A_GROUP_DIM, axis=1))
    return prompt, sample


def kernel(x_prompt, x_sample, cache_k, cache_v, cache_kidx, page_table, w_in, ln_v_g, ln_v_b, w_s, b_s,
           w_pa, w_pb, w_o, ln1_g, ln1_b, w_gate, w_up, w_down, ln2_g, ln2_b):
    depth = w_in.shape[0]
    alpha = (2 * depth) ** 0.25
    b, s, _ = x_prompt.shape
    db, t_new, _ = x_sample.shape
    n_pages = page_table.shape[1]
    n_pool = cache_k.shape[1]
    past = n_pages * PAGE_SIZE
    assert t_new <= CHUNK and CHUNK % t_new == 0 and t_new <= NEW_ROWS
    tri = jnp.triu(jnp.ones((KEY_CHUNK, KEY_CHUNK), BF16))
    tril = jnp.tril(jnp.ones((KEY_CHUNK, KEY_CHUNK), BF16))

    y_p = x_prompt.reshape(b * s, D_MODEL)
    y_s = x_sample.reshape(db * t_new, D_MODEL)
    outs = [[] for _ in range(7)]
    for layer in range(depth):
        wp, ws = _prepare_weights(layer, w_in, ln_v_g, ln_v_b, w_s, b_s, w_pa, w_pb, w_o, ln1_g, ln1_b,
                                  w_gate, w_up, w_down, ln2_g, ln2_b, t_new)

        a, sgb, q, k_p, v_p, kb, vt, qi, kw, kn = _input_proj(y_p, wp, chunk_len=CHUNK, emit_vnorm=False,
                                                              seq_len=s)
        seq = lambda arr: arr.reshape(b, s, arr.shape[-1])
        bo = _prompt_attn(seq(q), seq(qi), seq(kw), seq(kb), seq(kn), vt, tril)
        y_p = _merge_ffn(y_p, a, sgb, bo.reshape(b * s, D_MODEL), wp, alpha=alpha, name="merge_ffn_prompt")
        outs[0].append(k_p.reshape(b, s, N_KV_HEADS, HEAD_DIM))
        outs[1].append(v_p.reshape(b, s, N_KV_HEADS, HEAD_DIM))
        outs[2].append(kw[:, :IDX_DIM].reshape(b, s, IDX_DIM))

        a, sgb, q, k_s, v_s, kb, vb, qi, kw, _, vn = _input_proj(y_s, ws, chunk_len=t_new, emit_vnorm=True)
        bo = _sample_mixer_b(page_table, q, qi, kw, kb, vb, cache_k[layer], cache_v[layer], cache_kidx[layer],
                             tri)
        y_s = _merge_ffn(y_s, a, sgb, bo, ws, alpha=alpha, name="merge_ffn_sample")
        outs[3].append(k_s.reshape(db, t_new, N_KV_HEADS, HEAD_DIM))
        outs[4].append(v_s.reshape(db, t_new, N_KV_HEADS, HEAD_DIM))
        outs[5].append(kw[:, :IDX_DIM].reshape(db, t_new, IDX_DIM))
        outs[6].append(vn.reshape(db, t_new, D_MODEL))

    return (y_p.reshape(b, s, D_MODEL), y_s.reshape(db, t_new, D_MODEL), *[jnp.stack(o) for o in outs])
```

```python
import functools
import math

import jax
import jax.numpy as jnp
from jax import lax
from jax.experimental import pallas as pl
from jax.experimental.pallas import tpu as pltpu
from jax.experimental.pallas import tpu_sc as plsc

D_MODEL = 1024
CHUNK = 128
A_GROUPS = 8
A_GROUP_DIM = D_MODEL // A_GROUPS
N_HEADS = 8
N_KV_HEADS = 4
Q_PER_KV = N_HEADS // N_KV_HEADS
HEAD_DIM = D_MODEL // N_HEADS
KV_DIM = N_KV_HEADS * HEAD_DIM
N_IDX_HEADS = 8
IDX_DIM = 64
TOPK_MAX = 256
PAGE_SIZE = 128
LN_EPS = 1e-5

LANES = 128
SUBLANES = 8
BF16_SUBLANES = 16
VMEM_LIMIT_BYTES = 56 * 1024 * 1024

TOKEN_TILE = 512
Q_TILE = 512
KEY_CHUNK = 512
SAMPLE_ROWS = 64
SC_CORES = 2
SC_SUBCORES = 16
SC_LANES = 16
SC_GATHER_WINDOW = 64
NEW_ROWS = 128

F32 = jnp.float32
BF16 = jnp.bfloat16
NEG_INF = float("-inf")
F32_LOWEST = float(jnp.finfo(jnp.float32).min)
M_INIT = -1e30
LOG2_E = math.log2(math.e)
BOUND_SLACK = 1.01
BOUND_FLOOR = 1e-6
BF16_ROUND_UP = 1.0 + 2.0 ** -7
DENOM_FLOOR = 1e-30
INT32_MIN = -(2 ** 31)
KEY_HALF_ULP = 2 ** 15


def _dot(a, b):
    return jnp.dot(a, b, preferred_element_type=F32)


def _dot_nt(a, b):
    return lax.dot_general(a, b, (((1,), (1,)), ((), ())), preferred_element_type=F32)


def _gelu(x):
    c = math.sqrt(2.0 / math.pi)
    return x * (0.5 * (1.0 + jnp.tanh(c * (x + 0.044715 * (x * x * x)))))


def _sigmoid(x):
    return 1.0 / (1.0 + jnp.exp(-x))


def _layer_norm(x, g, b):
    mu = jnp.mean(x, axis=-1, keepdims=True)
    xc = x - mu
    var = jnp.mean(xc * xc, axis=-1, keepdims=True)
    return xc * lax.rsqrt(var + LN_EPS) * g + b


def _key_to_float(u):
    c = u ^ jnp.int32(INT32_MIN)
    bits = c ^ ((c >> 31) & jnp.int32(0x7FFFFFFF))
    return lax.bitcast_convert_type(bits, F32)


def _resident(shape):
    nd = len(shape)
    return pl.BlockSpec(shape, lambda *_: (0,) * nd, pipeline_mode=pl.Buffered(1))


def _input_proj_kernel(x_ref, wuv_ref, wq_ref, wkv_ref, wqi_ref, wkw_ref, wg_ref, lng_ref, lnb_ref,
                       wmix_ref, bmix_ref, wpa_ref,
                       a_ref, sgb_ref, q_ref, k_ref, v_ref, kb_ref, vb_ref, qi_ref, kw_ref, kn_ref, qn0_ref, qn1_ref,
                       *rest,
                       chunk_len, emit_vnorm, transpose_v, q_scale):
    if emit_vnorm:
        vn_ref, aout_ref = rest
    else:
        (aout_ref,) = rest
    tm = x_ref.shape[0]
    xb = x_ref[...].astype(BF16)

    zuv = _dot(xb, wuv_ref[...])
    u = _gelu(zuv[:, :D_MODEL])
    vn = _layer_norm(_gelu(zuv[:, D_MODEL:]), lng_ref[...], lnb_ref[...])
    if emit_vnorm:
        vn_ref[...] = vn
    vnb = vn.astype(BF16)

    row = lax.broadcasted_iota(jnp.int32, (CHUNK, CHUNK), 0)
    col = lax.broadcasted_iota(jnp.int32, (CHUNK, CHUNK), 1)
    same_chunk = (row & ~(chunk_len - 1)) == (col & ~(chunk_len - 1))
    mix_mask = (col <= row) & same_chunk
    for g in range(A_GROUPS):
        gs = slice(g * A_GROUP_DIM, (g + 1) * A_GROUP_DIM)
        wm = jnp.where(mix_mask, wmix_ref[g], jnp.zeros((), BF16))
        for c in range(tm // CHUNK):
            rs = slice(c * CHUNK, (c + 1) * CHUNK)
            s = _dot(wm, vnb[rs, gs]) + bmix_ref[:, gs]
            aout_ref[rs, gs] = (u[rs, gs] * s).astype(BF16)

    pa = _dot(aout_ref[...], wpa_ref[...])
    gates = _dot(xb, wg_ref[...])
    a_ref[...] = (_sigmoid(gates[:, :D_MODEL]) * pa).astype(BF16)
    sgb_ref[...] = _sigmoid(gates[:, D_MODEL:]).astype(BF16)

    qb = (_dot(xb, wq_ref[...]) * q_scale).astype(BF16)
    q_ref[...] = qb
    lane = lax.broadcasted_iota(jnp.int32, (tm, LANES), 1)
    qsq = qb.astype(F32)
    qsq = qsq * qsq
    for i, qn_ref in enumerate((qn0_ref, qn1_ref)):
        qn = jnp.zeros((tm, LANES), F32)
        for g in range(N_KV_HEADS):
            h = g * Q_PER_KV + i
            qn = jnp.where(lane == g, jnp.sum(qsq[:, h * HEAD_DIM:(h + 1) * HEAD_DIM], axis=-1, keepdims=True), qn)
        qn_ref[...] = jnp.sqrt(qn)
    kv = _dot(xb, wkv_ref[...])
    for h in range(N_KV_HEADS):
        k_ref[pl.ds(h, tm, stride=N_KV_HEADS), :] = kv[:, h * HEAD_DIM:(h + 1) * HEAD_DIM]
        v_ref[pl.ds(h, tm, stride=N_KV_HEADS), :] = kv[:, KV_DIM + h * HEAD_DIM:KV_DIM + (h + 1) * HEAD_DIM]
    kb = kv[:, :KV_DIM].astype(BF16)
    kb_ref[...] = kb
    ksq = kb.astype(F32)
    ksq = ksq * ksq
    kn = jnp.zeros((tm, LANES), F32)
    for h in range(N_KV_HEADS):
        kn = jnp.where(lane == h, jnp.sum(ksq[:, h * HEAD_DIM:(h + 1) * HEAD_DIM], axis=-1, keepdims=True), kn)
    kn_ref[...] = kn
    if transpose_v:
        vb_ref[0] = kv[:, KV_DIM:].T.astype(BF16)
    else:
        vb_ref[...] = kv[:, KV_DIM:].astype(BF16)
    qi_ref[...] = _dot(xb, wqi_ref[...]).astype(BF16)
    kw_ref[...] = _dot(xb, wkw_ref[...])


def _input_proj(x, wts, *, chunk_len, emit_vnorm, seq_len=None):
    t = x.shape[0]
    tm = min(TOKEN_TILE, t)
    assert t % tm == 0 and tm % CHUNK == 0
    row_spec = lambda n, rows_per_token=1: pl.BlockSpec((tm * rows_per_token, n), lambda i: (i, 0))
    out_shapes = [
        jax.ShapeDtypeStruct((t, D_MODEL), BF16),
        jax.ShapeDtypeStruct((t, D_MODEL), BF16),
        jax.ShapeDtypeStruct((t, D_MODEL), BF16),
        jax.ShapeDtypeStruct((t * N_KV_HEADS, HEAD_DIM), F32),
        jax.ShapeDtypeStruct((t * N_KV_HEADS, HEAD_DIM), F32),
        jax.ShapeDtypeStruct((t, KV_DIM), BF16),
        jax.ShapeDtypeStruct((t, KV_DIM), BF16),
        jax.ShapeDtypeStruct((t, N_IDX_HEADS * IDX_DIM), BF16),
        jax.ShapeDtypeStruct((t, LANES), F32),
        jax.ShapeDtypeStruct((t, LANES), F32),
        jax.ShapeDtypeStruct((t, LANES), F32),
        jax.ShapeDtypeStruct((t, LANES), F32),
    ]
    out_specs = [row_spec(s.shape[1], s.shape[0] // t) for s in out_shapes]
    if seq_len is not None:
        assert seq_len % tm == 0 and t % seq_len == 0
        tiles = seq_len // tm
        out_shapes[6] = jax.ShapeDtypeStruct((t // seq_len, KV_DIM, seq_len), BF16)
        out_specs[6] = pl.BlockSpec((1, KV_DIM, tm), lambda i: (i // tiles, 0, i % tiles))
    if emit_vnorm:
        out_shapes.append(jax.ShapeDtypeStruct((t, D_MODEL), F32))
        out_specs.append(row_spec(D_MODEL))
    weights = (wts["wuv"], wts["wq"], wts["wkv"], wts["wqi"], wts["wkw"], wts["wg"], wts["ln_v_g"],
               wts["ln_v_b"], wts["wmix"], wts["bmix"], wts["wpa"])
    return pl.pallas_call(
        functools.partial(_input_proj_kernel, chunk_len=chunk_len, emit_vnorm=emit_vnorm,
                          transpose_v=seq_len is not None,
                          q_scale=HEAD_DIM ** -0.5 * (LOG2_E if seq_len is not None else 1.0)),
        grid=(t // tm,),
        in_specs=[row_spec(D_MODEL)] + [_resident(w.shape) for w in weights],
        out_specs=out_specs,
        out_shape=out_shapes,
        scratch_shapes=[pltpu.VMEM((tm, D_MODEL), BF16)],
        compiler_params=pltpu.CompilerParams(dimension_semantics=("arbitrary",),
                                             vmem_limit_bytes=VMEM_LIMIT_BYTES),
        name="input_proj_sample" if emit_vnorm else "input_proj_prompt",
    )(x, *weights)


def _merge_ffn_kernel(x_ref, a_ref, sgb_ref, bo_ref, wpb_ref, wo_ref, ln1g_ref, ln1b_ref, wgate_ref,
                      wup_ref, wdown_ref, ln2g_ref, ln2b_ref, y_ref, *, alpha):
    pb = _dot(bo_ref[...], wpb_ref[...])
    merged = a_ref[...].astype(F32) + sgb_ref[...].astype(F32) * pb
    mix = _dot(merged.astype(BF16), wo_ref[...])
    x1 = _layer_norm(alpha * x_ref[...] + mix, ln1g_ref[...], ln1b_ref[...])
    x1b = x1.astype(BF16)
    hg = _dot(x1b, wgate_ref[...])
    hu = _dot(x1b, wup_ref[...])
    h = (hg * _sigmoid(hg)) * hu
    f = _dot(h.astype(BF16), wdown_ref[...])
    y_ref[...] = _layer_norm(alpha * x1 + f, ln2g_ref[...], ln2b_ref[...])


def _merge_ffn(x, a, sgb, bo, wts, *, alpha, name):
    t = x.shape[0]
    tm = min(TOKEN_TILE, t)
    assert t % tm == 0
    row_spec = pl.BlockSpec((tm, D_MODEL), lambda i: (i, 0))
    weights = (wts["wpb"], wts["wo"], wts["ln1_g"], wts["ln1_b"], wts["wgate"], wts["wup"], wts["wdown"],
               wts["ln2_g"], wts["ln2_b"])
    return pl.pallas_call(
        functools.partial(_merge_ffn_kernel, alpha=alpha),
        grid=(t // tm,),
        in_specs=[row_spec] * 4 + [_resident(w.shape) for w in weights],
        out_specs=row_spec,
        out_shape=jax.ShapeDtypeStruct((t, D_MODEL), F32),
        compiler_params=pltpu.CompilerParams(dimension_semantics=("arbitrary",),
                                             vmem_limit_bytes=VMEM_LIMIT_BYTES),
        name=name,
    )(x, a, sgb, bo, *weights)


def _lane_partial(x):
    acc = x[:, :LANES]
    for i in range(1, x.shape[1] // LANES):
        acc = acc + x[:, i * LANES:(i + 1) * LANES]
    return acc


def _bisect_key(count_ge_key, shape, topk, *, base=None, hi_bit=31, n_bits=32):
    def bit_body(i, carry):
        d, cnt_d = carry
        cand = d | lax.shift_left(jnp.int32(1), hi_bit - i)
        cnt = count_ge_key(cand if base is None else base + cand)
        take = cnt >= topk
        return jnp.where(take, cand, d), jnp.where(take, cnt, cnt_d)
    d, cnt = lax.fori_loop(0, n_bits, bit_body, (jnp.zeros(shape, jnp.int32), jnp.zeros(shape, F32)))
    return (d if base is None else base + d), cnt


def _kth_largest_key(count_ge, shape, topk):
    return _bisect_key(lambda key: count_ge(_key_to_float(key)), shape, topk)


def _sublane_partial(x):
    parts = [x[i * SUBLANES:(i + 1) * SUBLANES] for i in range(x.shape[0] // SUBLANES)]
    while len(parts) > 1:
        parts = [a + b for a, b in zip(parts[::2], parts[1::2])] + parts[len(parts) - len(parts) % 2:]
    return parts[0]


def _select_chunk_t(s, thr, need, carry, tril):
    gt = s > thr
    eq = s == thr
    eqf = eq.astype(F32)
    inc = _dot(tril, eqf.astype(BF16))
    before = carry + inc - eqf
    sel = gt | (eq & (before < need))
    return jnp.where(sel, 0.0, NEG_INF), carry + inc[inc.shape[0] - 1:]


def _threshold_bias(s, thr):
    return jnp.where(s >= thr, 0.0, NEG_INF)


def _select_chunk(s, thr, need, carry, tri):
    gt = s > thr
    eq = s == thr
    eqf = eq.astype(F32)
    inc = _dot(eqf.astype(BF16), tri)
    before = carry + inc - eqf
    sel = gt | (eq & (before < need))
    bias = jnp.where(sel, 0.0, NEG_INF)
    return bias, carry + inc[:, inc.shape[1] - 1:]


def _prompt_attn_kernel(q_ref, qi_ref, kwq_ref, qn0_ref, qn1_ref, kb_ref, kn_ref, vt_ref, kw_ref, tril_ref, o_ref,
                        sc_ref, sbf_ref, m_ref, acc_ref, *, topk):
    tq = q_ref.shape[1]
    ck = KEY_CHUNK
    j = pl.program_id(1)
    n_chunks = lax.shift_right_logical((j + 1) * tq + (ck - 1), int(math.log2(ck)))
    t = j * tq + lax.broadcasted_iota(jnp.int32, (1, tq), 1)
    chunk = lambda c: pl.ds(pl.multiple_of(c * ck, ck), ck)

    qi = qi_ref[0]
    w_t = kwq_ref[0].T[IDX_DIM:IDX_DIM + N_IDX_HEADS]
    w_t = (w_t * (N_IDX_HEADS ** -0.5)) * (IDX_DIM ** -0.5)
    qi_pairs = [jnp.concatenate([qi[:, h * IDX_DIM:(h + 1) * IDX_DIM] for h in (2 * p, 2 * p + 1)], axis=0)
                for p in range(N_IDX_HEADS // 2)]
    w_pairs = [jnp.concatenate([w_t[h:h + 1] for h in (2 * p, 2 * p + 1)], axis=1)
               for p in range(N_IDX_HEADS // 2)]

    def score_chunk(c, carry):
        kc = kw_ref[0, chunk(c), :][:, :IDX_DIM].astype(BF16)
        acc = jnp.zeros((ck, tq), F32)
        for p in range(N_IDX_HEADS // 2):
            d = jnp.maximum(_dot_nt(kc, qi_pairs[p]), 0.0) * w_pairs[p]
            acc = acc + (d[:, :tq] + d[:, tq:])
        kpos = c * ck + lax.broadcasted_iota(jnp.int32, (ck, 1), 0)
        masked = jnp.where(kpos <= t, acc, NEG_INF)
        sc_ref[chunk(c), :] = masked
        sbf_ref[chunk(c), :] = masked.astype(BF16)
        return carry

    lax.fori_loop(0, n_chunks, score_chunk, 0)

    def count_rounded(key):
        cand = _key_to_float(key).astype(BF16)
        n_acc = 4

        def body(c, accs):
            accs = list(accs)
            for i in range(ck // BF16_SUBLANES):
                rows = pl.ds(pl.multiple_of(c * ck + i * BF16_SUBLANES, BF16_SUBLANES), BF16_SUBLANES)
                accs[i % n_acc] = accs[i % n_acc] + jnp.where(sbf_ref[rows, :] >= cand, one, zero)
            return tuple(accs)

        one = jnp.ones((BF16_SUBLANES, tq), BF16)
        zero = jnp.zeros((BF16_SUBLANES, tq), BF16)
        accs = lax.fori_loop(0, n_chunks, body, (zero,) * n_acc)
        total = (accs[0].astype(F32) + accs[1].astype(F32)) + (accs[2].astype(F32) + accs[3].astype(F32))
        return jnp.sum(total, axis=0, keepdims=True)

    def count(pred):
        n_acc = 4

        def body(c, accs):
            accs = list(accs)
            for i in range(ck // SUBLANES):
                rows = pl.ds(pl.multiple_of(c * ck + i * SUBLANES, SUBLANES), SUBLANES)
                accs[i % n_acc] = accs[i % n_acc] + pred(sc_ref[rows, :]).astype(F32)
            return tuple(accs)

        zero = jnp.zeros((SUBLANES, tq), F32)
        accs = lax.fori_loop(0, n_chunks, body, (zero,) * n_acc)
        return jnp.sum((accs[0] + accs[1]) + (accs[2] + accs[3]), axis=0, keepdims=True)

    coarse, _ = _bisect_key(count_rounded, (1, tq), topk, n_bits=16)
    u, cnt_u = _bisect_key(lambda key: count(lambda s: s >= _key_to_float(key)), (1, tq), topk,
                           base=coarse - KEY_HALF_ULP, hi_bit=16, n_bits=17)
    small = t < topk
    thr = jnp.where(small, F32_LOWEST, _key_to_float(u))
    excess = jnp.where(cnt_u == 0.0, 1.0, cnt_u - topk)
    has_ties = jnp.max(jnp.where(small, 0.0, excess)) > 0.0

    @pl.when(has_ties)
    def _():
        need = topk - count(lambda s: s > thr)

        def select_chunk(c, carry):
            bias, carry = _select_chunk_t(sc_ref[chunk(c), :], thr, need, carry, tril_ref[...])
            sc_ref[chunk(c), :] = bias
            return carry

        lax.fori_loop(0, n_chunks, select_chunk, jnp.zeros((1, tq), F32))

    @pl.when(jnp.logical_not(has_ties))
    def _():
        def select_chunk(c, carry):
            sc_ref[chunk(c), :] = _threshold_bias(sc_ref[chunk(c), :], thr)
            return carry

        lax.fori_loop(0, n_chunks, select_chunk, 0)

    q = q_ref[0]
    ones = jnp.ones((BF16_SUBLANES, ck), BF16)
    q2s = [jnp.concatenate([q[:, (g * Q_PER_KV + i) * HEAD_DIM:(g * Q_PER_KV + i + 1) * HEAD_DIM]
                            for i in range(Q_PER_KV)], axis=0) for g in range(N_KV_HEADS)]

    def key_norm_chunk(c, acc):
        return jnp.maximum(acc, jnp.max(kn_ref[0, chunk(c), :], axis=0, keepdims=True))

    kn_max = lax.fori_loop(0, n_chunks, key_norm_chunk, jnp.zeros((1, LANES), F32))
    k_norm = jnp.sqrt(kn_max) * (BOUND_SLACK * BF16_ROUND_UP)
    bound = jnp.concatenate([qn0_ref[0] * k_norm, qn1_ref[0] * k_norm], axis=0) + BOUND_FLOOR
    shift = (-bound).astype(BF16)
    q2_shifted = [jnp.concatenate([q2s[g], shift], axis=1) for g in range(N_KV_HEADS)]
    lane_k = lax.broadcasted_iota(jnp.int32, (ck, HEAD_DIM), 1)
    pick = [jnp.where(lane_k == g, 1.0, 0.0).astype(BF16) for g in range(N_KV_HEADS)]
    acc_ref[...] = jnp.zeros(acc_ref.shape, F32)

    def attend_shifted(c, carry):
        bias = jnp.concatenate([sc_ref[chunk(c), :]] * Q_PER_KV, axis=1)
        for g in range(N_KV_HEADS):
            gs = slice(g * HEAD_DIM, (g + 1) * HEAD_DIM)
            k1 = jnp.concatenate([kb_ref[0, chunk(c), gs], pick[g]], axis=1)
            p = jnp.exp2(_dot_nt(k1, q2_shifted[g]) + bias).astype(BF16)
            v1 = jnp.concatenate([vt_ref[0, gs, chunk(c)], ones], axis=0)
            acc_ref[g] = acc_ref[g] + _dot(v1, p)
        return carry

    lax.fori_loop(0, n_chunks, attend_shifted, 0)
    denom_min = acc_ref[0][HEAD_DIM:HEAD_DIM + 1]
    for g in range(1, N_KV_HEADS):
        denom_min = jnp.minimum(denom_min, acc_ref[g][HEAD_DIM:HEAD_DIM + 1])
    underflow = jnp.logical_not(jnp.min(denom_min) > DENOM_FLOOR)

    @pl.when(underflow)
    def _():
        _attend_running_max(q2s, kb_ref, vt_ref, sc_ref, m_ref, acc_ref, n_chunks, chunk, ones)

    for g in range(N_KV_HEADS):
        acc = acc_ref[g]
        o_t = acc[:HEAD_DIM] / acc[HEAD_DIM:HEAD_DIM + 1]
        for i in range(Q_PER_KV):
            h = g * Q_PER_KV + i
            o_ref[0, :, h * HEAD_DIM:(h + 1) * HEAD_DIM] = o_t[:, i * tq:(i + 1) * tq].T.astype(BF16)


def _attend_running_max(q2s, kb_ref, vt_ref, sc_ref, m_ref, acc_ref, n_chunks, chunk, ones):
    m_ref[...] = jnp.full(m_ref.shape, M_INIT, F32)
    acc_ref[...] = jnp.zeros(acc_ref.shape, F32)

    def attend_chunk(c, carry):
        bias = jnp.concatenate([sc_ref[chunk(c), :]] * Q_PER_KV, axis=1)
        for g in range(N_KV_HEADS):
            gs = slice(g * HEAD_DIM, (g + 1) * HEAD_DIM)
            s = _dot_nt(kb_ref[0, chunk(c), gs], q2s[g]) + bias
            m_old = m_ref[g][:1]
            m_new = jnp.maximum(m_old, jnp.max(s, axis=0, keepdims=True))
            p = jnp.exp2(s - m_new).astype(BF16)
            v1 = jnp.concatenate([vt_ref[0, gs, chunk(c)], ones], axis=0)
            acc_ref[g] = jnp.exp2(m_old - m_new) * acc_ref[g] + _dot(v1, p)
            m_ref[g] = jnp.broadcast_to(m_new, m_ref.shape[1:])
        return carry

    lax.fori_loop(0, n_chunks, attend_chunk, 0)


def _prompt_attn(q, qi, kw, qn0, qn1, kb, kn, vt, tril):
    b, s, _ = q.shape
    tq = Q_TILE
    assert s % KEY_CHUNK == 0 and s % tq == 0 and tq % LANES == 0
    topk = min(TOPK_MAX, s // 4)
    blk = lambda n: pl.BlockSpec((1, tq, n), lambda i, j: (i, j, 0))
    seq = lambda n: pl.BlockSpec((1, s, n), lambda i, j: (i, 0, 0))
    return pl.pallas_call(
        functools.partial(_prompt_attn_kernel, topk=topk),
        grid=(b, s // tq),
        in_specs=[blk(D_MODEL), blk(N_IDX_HEADS * IDX_DIM), blk(LANES), blk(LANES), blk(LANES), seq(KV_DIM),
                  seq(LANES),
                  pl.BlockSpec((1, KV_DIM, s), lambda i, j: (i, 0, 0)), seq(LANES), _resident(tril.shape)],
        out_specs=blk(D_MODEL),
        out_shape=jax.ShapeDtypeStruct((b, s, D_MODEL), BF16),
        scratch_shapes=[pltpu.VMEM((s, tq), F32),
                        pltpu.VMEM((s, tq), BF16),
                        pltpu.VMEM((N_KV_HEADS, SUBLANES, Q_PER_KV * tq), F32),
                        pltpu.VMEM((N_KV_HEADS, HEAD_DIM + BF16_SUBLANES, Q_PER_KV * tq), F32)],
        compiler_params=pltpu.CompilerParams(dimension_semantics=("arbitrary", "arbitrary"),
                                             vmem_limit_bytes=VMEM_LIMIT_BYTES),
        name="prompt_attn",
    )(q, qi, kw, qn0, qn1, kb, kn, vt, kw, tril)


def _sample_score_kernel(pt_ref, qi_ref, w_ref, knew_ref, *refs, n_pages, t_new):
    del pt_ref
    page_refs = refs[:n_pages]
    sp_ref, sn_ref = refs[n_pages:]
    qi = qi_ref[0]
    w = (w_ref[0] * (N_IDX_HEADS ** -0.5)) * (IDX_DIM ** -0.5)

    def head_sum(dots):
        r = jnp.maximum(dots, 0.0) * w
        return [jnp.sum(r[i * N_IDX_HEADS:(i + 1) * N_IDX_HEADS], axis=0, keepdims=True)
                for i in range(t_new)]

    group = 4
    for p0 in range(0, n_pages, group):
        keys_t = jnp.concatenate([page_refs[p][0] for p in range(p0, p0 + group)], axis=1).astype(BF16)
        for i, r in enumerate(head_sum(_dot(qi, keys_t))):
            sp_ref[0, i:i + 1, p0 * PAGE_SIZE:(p0 + group) * PAGE_SIZE] = r

    knew = jnp.concatenate([knew_ref[0][:, :IDX_DIM],
                            jnp.zeros((LANES - knew_ref.shape[1], IDX_DIM), F32)], axis=0).astype(BF16)
    kpos = lax.broadcasted_iota(jnp.int32, (1, LANES), 1)
    for i, r in enumerate(head_sum(_dot_nt(qi, knew))):
        sn_ref[0, i:i + 1, :] = jnp.where(kpos <= i, r, NEG_INF)


def _sample_scores(page_table, qi32, w32, knew, cache_kidx_t):
    db, n_pages = page_table.shape
    t_new = qi32.shape[1] // N_IDX_HEADS
    assert n_pages % 4 == 0
    page_spec = lambda p: pl.BlockSpec((1, IDX_DIM, PAGE_SIZE), lambda b, pt: (pt[b, p], 0, 0))
    grid_spec = pltpu.PrefetchScalarGridSpec(
        num_scalar_prefetch=1,
        grid=(db,),
        in_specs=[pl.BlockSpec((1,) + qi32.shape[1:], lambda b, pt: (b, 0, 0)),
                  pl.BlockSpec((1,) + w32.shape[1:], lambda b, pt: (b, 0, 0)),
                  pl.BlockSpec((1,) + knew.shape[1:], lambda b, pt: (b, 0, 0))]
                 + [page_spec(p) for p in range(n_pages)],
        out_specs=[pl.BlockSpec((1, t_new, n_pages * PAGE_SIZE), lambda b, pt: (b, 0, 0)),
                   pl.BlockSpec((1, t_new, LANES), lambda b, pt: (b, 0, 0))],
    )
    return pl.pallas_call(
        functools.partial(_sample_score_kernel, n_pages=n_pages, t_new=t_new),
        grid_spec=grid_spec,
        out_shape=[jax.ShapeDtypeStruct((db, t_new, n_pages * PAGE_SIZE), F32),
                   jax.ShapeDtypeStruct((db, t_new, LANES), F32)],
        compiler_params=pltpu.CompilerParams(dimension_semantics=("arbitrary",),
                                             vmem_limit_bytes=VMEM_LIMIT_BYTES),
        name="sample_scores",
    )(page_table, qi32, w32, knew, *([cache_kidx_t] * n_pages))


def _sample_select_kernel(sp_ref, sn_ref, tri_ref, slot_ref, bn_ref, npast_ref, *, topk):
    rows, past = sp_ref.shape
    ck = KEY_CHUNK
    n_chunks = past // ck

    def count(pred):
        def body(c, cnt):
            s = sp_ref[:, pl.ds(pl.multiple_of(c * ck, ck), ck)]
            return cnt + _lane_partial(pred(s).astype(F32))
        part = lax.fori_loop(0, n_chunks, body, pred(sn_ref[...]).astype(F32))
        return jnp.sum(part, axis=-1, keepdims=True)

    u, _ = _kth_largest_key(lambda cand: count(lambda s: s >= cand), (rows, 1), topk)
    thr = _key_to_float(u)
    need = topk - count(lambda s: s > thr)

    def select_chunk(c, carry):
        ties, taken = carry
        cs = pl.ds(pl.multiple_of(c * ck, ck), ck)
        bias, ties = _select_chunk(sp_ref[:, cs], thr, need, ties, tri_ref[...])
        sel = (bias == 0.0).astype(F32)
        inc = _dot(sel.astype(BF16), tri_ref[...])
        slot_ref[:, cs] = jnp.where(sel > 0.0, taken + inc - 1.0, -1.0).astype(jnp.int32)
        return ties, taken + inc[:, ck - 1:]

    zero = jnp.zeros((rows, 1), F32)
    ties, taken = lax.fori_loop(0, n_chunks, select_chunk, (zero, zero))
    bias, _ = _select_chunk(sn_ref[...], thr, need, ties, tri_ref[:LANES, :LANES])
    bn_ref[...] = bias
    npast_ref[...] = jnp.broadcast_to(taken, npast_ref.shape)


def _sample_select(sp, sn, tri, *, topk):
    n, past = sp.shape
    rows = min(SAMPLE_ROWS, n)
    assert n % rows == 0 and past % KEY_CHUNK == 0
    lane_spec = pl.BlockSpec((rows, LANES), lambda i: (i, 0))
    return pl.pallas_call(
        functools.partial(_sample_select_kernel, topk=topk),
        grid=(n // rows,),
        in_specs=[pl.BlockSpec((rows, past), lambda i: (i, 0)), lane_spec, _resident(tri.shape)],
        out_specs=[pl.BlockSpec((rows, past), lambda i: (i, 0)), lane_spec, lane_spec],
        out_shape=[jax.ShapeDtypeStruct((n, past), jnp.int32), jax.ShapeDtypeStruct((n, LANES), F32),
                   jax.ShapeDtypeStruct((n, LANES), F32)],
        compiler_params=pltpu.CompilerParams(dimension_semantics=("arbitrary",),
                                             vmem_limit_bytes=VMEM_LIMIT_BYTES),
        name="sample_select",
    )(sp, sn, tri)


def _sample_gather(slot, page_table, table_k, table_v, *, topk):
    nq, n_pages, _ = slot.shape
    db = page_table.shape[0]
    t_new = nq // db
    workers = SC_CORES * SC_SUBCORES
    assert nq % workers == 0 and topk % SC_GATHER_WINDOW == 0 and PAGE_SIZE % SC_LANES == 0
    assert t_new & (t_new - 1) == 0
    per_worker = nq // workers
    row_shape = (N_KV_HEADS, HEAD_DIM)
    out = jax.ShapeDtypeStruct((nq, topk) + row_shape, F32)
    mesh = plsc.VectorSubcoreMesh(core_axis_name="c", subcore_axis_name="s")

    @functools.partial(
        pl.kernel, mesh=mesh, out_type=(out, out),
        scratch_types=[pltpu.VMEM((n_pages, PAGE_SIZE), jnp.int32),
                       pltpu.VMEM((n_pages,), jnp.int32),
                       pltpu.VMEM((topk,), jnp.int32),
                       pltpu.VMEM((SC_GATHER_WINDOW,) + row_shape, F32)],
        compiler_params=pltpu.CompilerParams(needs_layout_passes=False),
        name="sample_gather")
    def gather(slot_hbm, pt_hbm, tk_hbm, tv_hbm, kg_hbm, vg_hbm, slot_v, pt_v, idx_v, rows_v):
        worker = lax.axis_index("c") * SC_SUBCORES + lax.axis_index("s")
        lane = lax.iota(jnp.int32, SC_LANES)

        @pl.loop(0, per_worker)
        def _(i):
            q = worker * per_worker + i
            pltpu.sync_copy(slot_hbm.at[q], slot_v)
            pltpu.sync_copy(pt_hbm.at[lax.shift_right_logical(q, int(math.log2(t_new)))], pt_v)

            @pl.loop(0, topk // SC_LANES)
            def _(j):
                idx_v[pl.ds(j * SC_LANES, SC_LANES)] = jnp.zeros((SC_LANES,), jnp.int32)

            @pl.loop(0, n_pages)
            def _(p):
                base = plsc.load_gather(pt_v, [jnp.full((SC_LANES,), p, jnp.int32)]) * PAGE_SIZE
                for l in range(PAGE_SIZE // SC_LANES):
                    s = slot_v[p, pl.ds(l * SC_LANES, SC_LANES)]
                    plsc.store_scatter(idx_v, [s], base + (l * SC_LANES + lane), mask=s >= 0)

            for w in range(topk // SC_GATHER_WINDOW):
                win = pl.ds(w * SC_GATHER_WINDOW, SC_GATHER_WINDOW)
                for table, dst in ((tk_hbm, kg_hbm), (tv_hbm, vg_hbm)):
                    pltpu.sync_copy(table.at[idx_v.at[win]], rows_v)
                    pltpu.sync_copy(rows_v, dst.at[q, win])

    return gather(slot, page_table, table_k, table_v)


def _sample_attn_kernel(q_ref, kg_ref, vg_ref, knew_ref, vnew_ref, bn_ref, npast_ref, o_ref, *, t_new, topk):
    rows = t_new * Q_PER_KV
    n_keys = t_new * topk
    row_query = lax.shift_right_logical(lax.broadcasted_iota(jnp.int32, (rows, 1), 0), int(math.log2(Q_PER_KV)))
    col = lax.broadcasted_iota(jnp.int32, (1, n_keys), 1)
    col_query = lax.shift_right_logical(col, int(math.log2(topk)))
    col_slot = (col & (topk - 1)).astype(F32)
    own = (col_query == row_query) & (col_slot < npast_ref[0][:, :1])
    bias_past = jnp.where(own, 0.0, NEG_INF)
    bias_new = bn_ref[0]
    for h in range(N_KV_HEADS):
        hs = slice(h * HEAD_DIM, (h + 1) * HEAD_DIM)
        q = q_ref[0, h]
        keys = kg_ref[pl.ds(h, n_keys, stride=N_KV_HEADS), :].astype(BF16)
        vals = vg_ref[pl.ds(h, n_keys, stride=N_KV_HEADS), :].astype(BF16)
        s_past = _dot_nt(q, keys) + bias_past
        s_new = _dot_nt(q, knew_ref[0][:, hs]) + bias_new
        m = jnp.maximum(jnp.max(s_past, axis=-1, keepdims=True), jnp.max(s_new, axis=-1, keepdims=True))
        p_past = jnp.exp(s_past - m)
        p_new = jnp.exp(s_new - m)
        denom = jnp.sum(p_past, axis=-1, keepdims=True) + jnp.sum(p_new, axis=-1, keepdims=True)
        o = _dot(p_past.astype(BF16), vals) + _dot(p_new.astype(BF16), vnew_ref[0][:, hs])
        o_ref[0, h] = (o / denom).astype(BF16)


def _sample_attn(q8, kg, vg, knew, vnew, bias_new8, npast8, *, topk):
    db, _, rows, _ = q8.shape
    t_new = rows // Q_PER_KV
    assert topk & (topk - 1) == 0
    gathered = pl.BlockSpec((t_new * topk * N_KV_HEADS, HEAD_DIM), lambda b: (b, 0))
    per_seq = lambda arr: pl.BlockSpec((1,) + arr.shape[1:], lambda b: (b,) + (0,) * (arr.ndim - 1))
    return pl.pallas_call(
        functools.partial(_sample_attn_kernel, t_new=t_new, topk=topk),
        grid=(db,),
        in_specs=[per_seq(q8), gathered, gathered, per_seq(knew), per_seq(vnew), per_seq(bias_new8),
                  per_seq(npast8)],
        out_specs=per_seq(q8),
        out_shape=jax.ShapeDtypeStruct(q8.shape, BF16),
        compiler_params=pltpu.CompilerParams(dimension_semantics=("arbitrary",),
                                             vmem_limit_bytes=VMEM_LIMIT_BYTES),
        name="sample_attn",
    )(q8, kg, vg, knew, vnew, bias_new8, npast8)


def _sample_mixer_b(page_table, q, qi, kw, kb, vb, cache_k, cache_v, cache_kidx, tri):
    db, n_pages = page_table.shape
    n_pool = cache_k.shape[0]
    t_new = q.shape[0] // db
    past = n_pages * PAGE_SIZE
    topk = min(TOPK_MAX, (past + t_new) // 4)
    rows = t_new * N_IDX_HEADS
    qi32 = qi.reshape(db, rows, IDX_DIM)
    w32 = kw[:, IDX_DIM:IDX_DIM + N_IDX_HEADS].reshape(db, rows, 1)
    knew_idx = jnp.pad(kw.reshape(db, t_new, LANES), ((0, 0), (0, SUBLANES - t_new), (0, 0)))
    sp, sn = _sample_scores(page_table, qi32, w32, knew_idx, jnp.swapaxes(cache_kidx, 1, 2))
    slot, bias_n, npast = _sample_select(sp.reshape(db * t_new, past), sn.reshape(db * t_new, LANES), tri,
                                         topk=topk)
    kg, vg = _sample_gather(slot.reshape(db * t_new, n_pages, PAGE_SIZE), page_table,
                            cache_k.reshape(n_pool * PAGE_SIZE, N_KV_HEADS, HEAD_DIM),
                            cache_v.reshape(n_pool * PAGE_SIZE, N_KV_HEADS, HEAD_DIM), topk=topk)
    q8 = q.reshape(db, t_new, N_KV_HEADS, Q_PER_KV, HEAD_DIM).transpose(0, 2, 1, 3, 4)
    q8 = q8.reshape(db, N_KV_HEADS, t_new * Q_PER_KV, HEAD_DIM)
    per_row = lambda arr: jnp.repeat(arr.reshape(db, t_new, LANES), Q_PER_KV, axis=1)
    pad_new = lambda arr: jnp.pad(arr.reshape(db, t_new, KV_DIM), ((0, 0), (0, NEW_ROWS - t_new), (0, 0)))
    flat = lambda arr: arr.reshape(db * t_new * topk * N_KV_HEADS, HEAD_DIM)
    o8 = _sample_attn(q8, flat(kg), flat(vg), pad_new(kb), pad_new(vb), per_row(bias_n), per_row(npast),
                      topk=topk)
    o = o8.reshape(db, N_KV_HEADS, t_new, Q_PER_KV, HEAD_DIM).transpose(0, 2, 1, 3, 4)
    return o.reshape(db * t_new, D_MODEL)


def _prepare_weights(layer, w_in, ln_v_g, ln_v_b, w_s, b_s, w_pa, w_pb, w_o, ln1_g, ln1_b, w_gate, w_up,
                     w_down, ln2_g, ln2_b, t_new):
    w = w_in[layer]
    o_q = 2 * D_MODEL
    o_kv = o_q + D_MODEL
    o_qi = o_kv + 2 * KV_DIM
    o_ki = o_qi + N_IDX_HEADS * IDX_DIM
    o_g = o_ki + IDX_DIM + N_IDX_HEADS
    row = lambda v: v[layer].reshape(1, -1)
    reps = CHUNK // t_new
    shared = dict(
        wuv=w[:, :o_q].astype(BF16), wq=w[:, o_q:o_kv].astype(BF16), wkv=w[:, o_kv:o_qi].astype(BF16),
        wqi=w[:, o_qi:o_ki].astype(BF16),
        wkw=jnp.pad(w[:, o_ki:o_g], ((0, 0), (0, LANES - (o_g - o_ki)))).astype(BF16),
        wg=w[:, o_g:].astype(BF16),
        ln_v_g=row(ln_v_g), ln_v_b=row(ln_v_b), wpa=w_pa[layer].astype(BF16),
        wpb=w_pb[layer].astype(BF16), wo=w_o[layer].astype(BF16), ln1_g=row(ln1_g), ln1_b=row(ln1_b),
        wgate=w_gate[layer].astype(BF16), wup=w_up[layer].astype(BF16), wdown=w_down[layer].astype(BF16),
        ln2_g=row(ln2_g), ln2_b=row(ln2_b))
    prompt = dict(shared, wmix=w_s[layer].astype(BF16),
                  bmix=jnp.repeat(b_s[layer].T, A_GROUP_DIM, axis=1))
    sample = dict(shared, wmix=jnp.tile(w_s[layer][:, :t_new, :t_new], (1, reps, reps)).astype(BF16),
                  bmix=jnp.repeat(jnp.tile(b_s[layer][:, :t_new].T, (reps, 1)), A_GROUP_DIM, axis=1))
    return prompt, sample


def kernel(x_prompt, x_sample, cache_k, cache_v, cache_kidx, page_table, w_in, ln_v_g, ln_v_b, w_s, b_s,
           w_pa, w_pb, w_o, ln1_g, ln1_b, w_gate, w_up, w_down, ln2_g, ln2_b):
    depth = w_in.shape[0]
    alpha = (2 * depth) ** 0.25
    b, s, _ = x_prompt.shape
    db, t_new, _ = x_sample.shape
    n_pages = page_table.shape[1]
    n_pool = cache_k.shape[1]
    past = n_pages * PAGE_SIZE
    assert t_new <= CHUNK and CHUNK % t_new == 0 and t_new <= NEW_ROWS
    tri = jnp.triu(jnp.ones((KEY_CHUNK, KEY_CHUNK), BF16))
    tril = jnp.tril(jnp.ones((KEY_CHUNK, KEY_CHUNK), BF16))

    y_p = x_prompt.reshape(b * s, D_MODEL)
    y_s = x_sample.reshape(db * t_new, D_MODEL)
    outs = [[] for _ in range(7)]
    for layer in range(depth):
        wp, ws = _prepare_weights(layer, w_in, ln_v_g, ln_v_b, w_s, b_s, w_pa, w_pb, w_o, ln1_g, ln1_b,
                                  w_gate, w_up, w_down, ln2_g, ln2_b, t_new)

        a, sgb, q, k_p, v_p, kb, vt, qi, kw, kn, qn0, qn1 = _input_proj(y_p, wp, chunk_len=CHUNK,
                                                                        emit_vnorm=False, seq_len=s)
        seq = lambda arr: arr.reshape(b, s, arr.shape[-1])
        bo = _prompt_attn(seq(q), seq(qi), seq(kw), seq(qn0), seq(qn1), seq(kb), seq(kn), vt, tril)
        y_p = _merge_ffn(y_p, a, sgb, bo.reshape(b * s, D_MODEL), wp, alpha=alpha, name="merge_ffn_prompt")
        outs[0].append(k_p.reshape(b, s, N_KV_HEADS, HEAD_DIM))
        outs[1].append(v_p.reshape(b, s, N_KV_HEADS, HEAD_DIM))
        outs[2].append(kw[:, :IDX_DIM].reshape(b, s, IDX_DIM))

        a, sgb, q, k_s, v_s, kb, vb, qi, kw, _, _, _, vn = _input_proj(y_s, ws, chunk_len=t_new, emit_vnorm=True)
        bo = _sample_mixer_b(page_table, q, qi, kw, kb, vb, cache_k[layer], cache_v[layer], cache_kidx[layer],
                             tri)
        y_s = _merge_ffn(y_s, a, sgb, bo, ws, alpha=alpha, name="merge_ffn_sample")
        outs[3].append(k_s.reshape(db, t_new, N_KV_HEADS, HEAD_DIM))
        outs[4].append(v_s.reshape(db, t_new, N_KV_HEADS, HEAD_DIM))
        outs[5].append(kw[:, :IDX_DIM].reshape(db, t_new, IDX_DIM))
        outs[6].append(vn.reshape(db, t_new, D_MODEL))

    return (y_p.reshape(b, s, D_MODEL), y_s.reshape(db, t_new, D_MODEL), *[jnp.stack(o) for o in outs])
```

```python
import functools
import math

import jax
import jax.numpy as jnp
from jax import lax
from jax.experimental import pallas as pl
from jax.experimental.pallas import tpu as pltpu
from jax.experimental.pallas import tpu_sc as plsc

D_MODEL = 1024
CHUNK = 128
A_GROUPS = 8
A_GROUP_DIM = D_MODEL // A_GROUPS
N_HEADS = 8
N_KV_HEADS = 4
Q_PER_KV = N_HEADS // N_KV_HEADS
HEAD_DIM = D_MODEL // N_HEADS
KV_DIM = N_KV_HEADS * HEAD_DIM
N_IDX_HEADS = 8
IDX_DIM = 64
TOPK_MAX = 256
PAGE_SIZE = 128
LN_EPS = 1e-5

LANES = 128
SUBLANES = 8
BF16_SUBLANES = 16
VMEM_LIMIT_BYTES = 56 * 1024 * 1024

TOKEN_TILE = 512
Q_TILE = 512
KEY_CHUNK = 512
SAMPLE_ROWS = 64
SC_CORES = 2
SC_SUBCORES = 16
SC_LANES = 16
SC_GATHER_WINDOW = 64
NEW_ROWS = 128

F32 = jnp.float32
BF16 = jnp.bfloat16
NEG_INF = float("-inf")
F32_LOWEST = float(jnp.finfo(jnp.float32).min)
M_INIT = -1e30
LOG2_E = math.log2(math.e)
BOUND_SLACK = 1.01
BOUND_FLOOR = 1e-6
BF16_ROUND_UP = 1.0 + 2.0 ** -7
DENOM_FLOOR = 1e-30
INT32_MIN = -(2 ** 31)
KEY_HALF_ULP = 2 ** 15


def _dot(a, b):
    return jnp.dot(a, b, preferred_element_type=F32)


def _dot_nt(a, b):
    return lax.dot_general(a, b, (((1,), (1,)), ((), ())), preferred_element_type=F32)


def _gelu(x):
    c = math.sqrt(2.0 / math.pi)
    return x * (0.5 * (1.0 + jnp.tanh(c * (x + 0.044715 * (x * x * x)))))


def _sigmoid(x):
    return 1.0 / (1.0 + jnp.exp(-x))


def _layer_norm(x, g, b):
    mu = jnp.mean(x, axis=-1, keepdims=True)
    xc = x - mu
    var = jnp.mean(xc * xc, axis=-1, keepdims=True)
    return xc * lax.rsqrt(var + LN_EPS) * g + b


def _key_to_float(u):
    c = u ^ jnp.int32(INT32_MIN)
    bits = c ^ ((c >> 31) & jnp.int32(0x7FFFFFFF))
    return lax.bitcast_convert_type(bits, F32)


def _resident(shape):
    nd = len(shape)
    return pl.BlockSpec(shape, lambda *_: (0,) * nd, pipeline_mode=pl.Buffered(1))


def _input_proj_kernel(x_ref, wuv_ref, wq_ref, wkv_ref, wqi_ref, wkw_ref, wg_ref, lng_ref, lnb_ref,
                       wmix_ref, bmix_ref, wpa_ref,
                       a_ref, sgb_ref, q_ref, k_ref, v_ref, kb_ref, vb_ref, qi_ref, kw_ref, kn_ref, qn0_ref, qn1_ref,
                       *rest,
                       chunk_len, emit_vnorm, transpose_v, q_scale):
    if emit_vnorm:
        vn_ref, aout_ref = rest
    else:
        (aout_ref,) = rest
    tm = x_ref.shape[0]
    xb = x_ref[...].astype(BF16)

    zuv = _dot(xb, wuv_ref[...])
    u = _gelu(zuv[:, :D_MODEL])
    vn = _layer_norm(_gelu(zuv[:, D_MODEL:]), lng_ref[...], lnb_ref[...])
    if emit_vnorm:
        vn_ref[...] = vn
    vnb = vn.astype(BF16)

    row = lax.broadcasted_iota(jnp.int32, (CHUNK, CHUNK), 0)
    col = lax.broadcasted_iota(jnp.int32, (CHUNK, CHUNK), 1)
    same_chunk = (row & ~(chunk_len - 1)) == (col & ~(chunk_len - 1))
    mix_mask = (col <= row) & same_chunk
    for g in range(A_GROUPS):
        gs = slice(g * A_GROUP_DIM, (g + 1) * A_GROUP_DIM)
        wm = jnp.where(mix_mask, wmix_ref[g], jnp.zeros((), BF16))
        for c in range(tm // CHUNK):
            rs = slice(c * CHUNK, (c + 1) * CHUNK)
            s = _dot(wm, vnb[rs, gs]) + bmix_ref[:, gs]
            aout_ref[rs, gs] = (u[rs, gs] * s).astype(BF16)

    pa = _dot(aout_ref[...], wpa_ref[...])
    gates = _dot(xb, wg_ref[...])
    a_ref[...] = (_sigmoid(gates[:, :D_MODEL]) * pa).astype(BF16)
    sgb_ref[...] = _sigmoid(gates[:, D_MODEL:]).astype(BF16)

    qb = (_dot(xb, wq_ref[...]) * q_scale).astype(BF16)
    q_ref[...] = qb
    lane = lax.broadcasted_iota(jnp.int32, (tm, LANES), 1)
    qsq = qb.astype(F32)
    qsq = qsq * qsq
    for i, qn_ref in enumerate((qn0_ref, qn1_ref)):
        qn = jnp.zeros((tm, LANES), F32)
        for g in range(N_KV_HEADS):
            h = g * Q_PER_KV + i
            qn = jnp.where(lane == g, jnp.sum(qsq[:, h * HEAD_DIM:(h + 1) * HEAD_DIM], axis=-1, keepdims=True), qn)
        qn_ref[...] = jnp.sqrt(qn)
    kv = _dot(xb, wkv_ref[...])
    for h in range(N_KV_HEADS):
        k_ref[pl.ds(h, tm, stride=N_KV_HEADS), :] = kv[:, h * HEAD_DIM:(h + 1) * HEAD_DIM]
        v_ref[pl.ds(h, tm, stride=N_KV_HEADS), :] = kv[:, KV_DIM + h * HEAD_DIM:KV_DIM + (h + 1) * HEAD_DIM]
    kb = kv[:, :KV_DIM].astype(BF16)
    kb_ref[...] = kb
    ksq = kb.astype(F32)
    ksq = ksq * ksq
    kn = jnp.zeros((tm, LANES), F32)
    for h in range(N_KV_HEADS):
        kn = jnp.where(lane == h, jnp.sum(ksq[:, h * HEAD_DIM:(h + 1) * HEAD_DIM], axis=-1, keepdims=True), kn)
    kn_ref[...] = kn
    if transpose_v:
        vb_ref[0] = kv[:, KV_DIM:].T.astype(BF16)
    else:
        vb_ref[...] = kv[:, KV_DIM:].astype(BF16)
    qi_ref[...] = _dot(xb, wqi_ref[...]).astype(BF16)
    kw_ref[...] = _dot(xb, wkw_ref[...])


def _input_proj(x, wts, *, chunk_len, emit_vnorm, seq_len=None):
    t = x.shape[0]
    tm = min(TOKEN_TILE, t)
    assert t % tm == 0 and tm % CHUNK == 0
    row_spec = lambda n, rows_per_token=1: pl.BlockSpec((tm * rows_per_token, n), lambda i: (i, 0))
    out_shapes = [
        jax.ShapeDtypeStruct((t, D_MODEL), BF16),
        jax.ShapeDtypeStruct((t, D_MODEL), BF16),
        jax.ShapeDtypeStruct((t, D_MODEL), BF16),
        jax.ShapeDtypeStruct((t * N_KV_HEADS, HEAD_DIM), F32),
        jax.ShapeDtypeStruct((t * N_KV_HEADS, HEAD_DIM), F32),
        jax.ShapeDtypeStruct((t, KV_DIM), BF16),
        jax.ShapeDtypeStruct((t, KV_DIM), BF16),
        jax.ShapeDtypeStruct((t, N_IDX_HEADS * IDX_DIM), BF16),
        jax.ShapeDtypeStruct((t, LANES), F32),
        jax.ShapeDtypeStruct((t, LANES), F32),
        jax.ShapeDtypeStruct((t, LANES), F32),
        jax.ShapeDtypeStruct((t, LANES), F32),
    ]
    out_specs = [row_spec(s.shape[1], s.shape[0] // t) for s in out_shapes]
    if seq_len is not None:
        assert seq_len % tm == 0 and t % seq_len == 0
        tiles = seq_len // tm
        out_shapes[6] = jax.ShapeDtypeStruct((t // seq_len, KV_DIM, seq_len), BF16)
        out_specs[6] = pl.BlockSpec((1, KV_DIM, tm), lambda i: (i // tiles, 0, i % tiles))
    if emit_vnorm:
        out_shapes.append(jax.ShapeDtypeStruct((t, D_MODEL), F32))
        out_specs.append(row_spec(D_MODEL))
    weights = (wts["wuv"], wts["wq"], wts["wkv"], wts["wqi"], wts["wkw"], wts["wg"], wts["ln_v_g"],
               wts["ln_v_b"], wts["wmix"], wts["bmix"], wts["wpa"])
    return pl.pallas_call(
        functools.partial(_input_proj_kernel, chunk_len=chunk_len, emit_vnorm=emit_vnorm,
                          transpose_v=seq_len is not None,
                          q_scale=HEAD_DIM ** -0.5 * (LOG2_E if seq_len is not None else 1.0)),
        grid=(t // tm,),
        in_specs=[row_spec(D_MODEL)] + [_resident(w.shape) for w in weights],
        out_specs=out_specs,
        out_shape=out_shapes,
        scratch_shapes=[pltpu.VMEM((tm, D_MODEL), BF16)],
        compiler_params=pltpu.CompilerParams(dimension_semantics=("arbitrary",),
                                             vmem_limit_bytes=VMEM_LIMIT_BYTES),
        name="input_proj_sample" if emit_vnorm else "input_proj_prompt",
    )(x, *weights)


def _merge_ffn_kernel(x_ref, a_ref, sgb_ref, bo_ref, wpb_ref, wo_ref, ln1g_ref, ln1b_ref, wgate_ref,
                      wup_ref, wdown_ref, ln2g_ref, ln2b_ref, y_ref, *, alpha):
    pb = _dot(bo_ref[...], wpb_ref[...])
    merged = a_ref[...].astype(F32) + sgb_ref[...].astype(F32) * pb
    mix = _dot(merged.astype(BF16), wo_ref[...])
    x1 = _layer_norm(alpha * x_ref[...] + mix, ln1g_ref[...], ln1b_ref[...])
    x1b = x1.astype(BF16)
    hg = _dot(x1b, wgate_ref[...])
    hu = _dot(x1b, wup_ref[...])
    h = (hg * _sigmoid(hg)) * hu
    f = _dot(h.astype(BF16), wdown_ref[...])
    y_ref[...] = _layer_norm(alpha * x1 + f, ln2g_ref[...], ln2b_ref[...])


def _merge_ffn(x, a, sgb, bo, wts, *, alpha, name):
    t = x.shape[0]
    tm = min(TOKEN_TILE, t)
    assert t % tm == 0
    row_spec = pl.BlockSpec((tm, D_MODEL), lambda i: (i, 0))
    weights = (wts["wpb"], wts["wo"], wts["ln1_g"], wts["ln1_b"], wts["wgate"], wts["wup"], wts["wdown"],
               wts["ln2_g"], wts["ln2_b"])
    return pl.pallas_call(
        functools.partial(_merge_ffn_kernel, alpha=alpha),
        grid=(t // tm,),
        in_specs=[row_spec] * 4 + [_resident(w.shape) for w in weights],
        out_specs=row_spec,
        out_shape=jax.ShapeDtypeStruct((t, D_MODEL), F32),
        compiler_params=pltpu.CompilerParams(dimension_semantics=("arbitrary",),
                                             vmem_limit_bytes=VMEM_LIMIT_BYTES),
        name=name,
    )(x, a, sgb, bo, *weights)


def _lane_partial(x):
    acc = x[:, :LANES]
    for i in range(1, x.shape[1] // LANES):
        acc = acc + x[:, i * LANES:(i + 1) * LANES]
    return acc


def _bisect_key(count_ge_key, shape, topk, *, base=None, hi_bit=31, n_bits=32):
    def bit_body(i, carry):
        d, cnt_d = carry
        cand = d | lax.shift_left(jnp.int32(1), hi_bit - i)
        cnt = count_ge_key(cand if base is None else base + cand)
        take = cnt >= topk
        return jnp.where(take, cand, d), jnp.where(take, cnt, cnt_d)
    d, cnt = lax.fori_loop(0, n_bits, bit_body, (jnp.zeros(shape, jnp.int32), jnp.zeros(shape, F32)))
    return (d if base is None else base + d), cnt


def _kth_largest_key(count_ge, shape, topk):
    return _bisect_key(lambda key: count_ge(_key_to_float(key)), shape, topk)


def _sublane_partial(x):
    parts = [x[i * SUBLANES:(i + 1) * SUBLANES] for i in range(x.shape[0] // SUBLANES)]
    while len(parts) > 1:
        parts = [a + b for a, b in zip(parts[::2], parts[1::2])] + parts[len(parts) - len(parts) % 2:]
    return parts[0]


def _select_chunk_t(s, thr, need, carry, tril):
    gt = s > thr
    eq = s == thr
    eqf = eq.astype(F32)
    inc = _dot(tril, eqf.astype(BF16))
    before = carry + inc - eqf
    sel = gt | (eq & (before < need))
    return jnp.where(sel, 0.0, NEG_INF), carry + inc[inc.shape[0] - 1:]


def _threshold_bias(s, thr):
    return jnp.where(s >= thr, 0.0, NEG_INF)


def _select_chunk(s, thr, need, carry, tri):
    gt = s > thr
    eq = s == thr
    eqf = eq.astype(F32)
    inc = _dot(eqf.astype(BF16), tri)
    before = carry + inc - eqf
    sel = gt | (eq & (before < need))
    bias = jnp.where(sel, 0.0, NEG_INF)
    return bias, carry + inc[:, inc.shape[1] - 1:]


def _prompt_attn_kernel(q_ref, qi_ref, kwq_ref, qn0_ref, qn1_ref, kb_ref, kn_ref, vt_ref, kw_ref, tril_ref, o_ref,
                        sc_ref, sbf_ref, m_ref, acc_ref, *, topk):
    tq = q_ref.shape[1]
    ck = KEY_CHUNK
    j = pl.program_id(1)
    n_chunks = lax.shift_right_logical((j + 1) * tq + (ck - 1), int(math.log2(ck)))
    t = j * tq + lax.broadcasted_iota(jnp.int32, (1, tq), 1)
    chunk = lambda c: pl.ds(pl.multiple_of(c * ck, ck), ck)

    qi = qi_ref[0]
    w_t = kwq_ref[0].T[IDX_DIM:IDX_DIM + N_IDX_HEADS]
    w_t = (w_t * (N_IDX_HEADS ** -0.5)) * (IDX_DIM ** -0.5)
    qi_pairs = [jnp.concatenate([qi[:, h * IDX_DIM:(h + 1) * IDX_DIM] for h in (2 * p, 2 * p + 1)], axis=0)
                for p in range(N_IDX_HEADS // 2)]
    w_pairs = [jnp.concatenate([w_t[h:h + 1] for h in (2 * p, 2 * p + 1)], axis=1)
               for p in range(N_IDX_HEADS // 2)]

    def score_chunk(c, carry):
        half = ck // 2
        for r in range(2):
            rows = pl.ds(pl.multiple_of(c * ck + r * half, half), half)
            kc = kw_ref[0, rows, :][:, :IDX_DIM].astype(BF16)
            acc = jnp.zeros((half, tq), F32)
            for p in range(N_IDX_HEADS // 2):
                d = jnp.maximum(_dot_nt(kc, qi_pairs[p]), 0.0) * w_pairs[p]
                acc = acc + (d[:, :tq] + d[:, tq:])
            kpos = c * ck + r * half + lax.broadcasted_iota(jnp.int32, (half, 1), 0)
            masked = jnp.where(kpos <= t, acc, NEG_INF)
            sc_ref[rows, :] = masked
            sbf_ref[rows, :] = masked.astype(BF16)
        return carry

    lax.fori_loop(0, n_chunks, score_chunk, 0)

    def count_rounded(key):
        cand = _key_to_float(key).astype(BF16)
        n_acc = 4

        def body(c, accs):
            accs = list(accs)
            for i in range(ck // BF16_SUBLANES):
                rows = pl.ds(pl.multiple_of(c * ck + i * BF16_SUBLANES, BF16_SUBLANES), BF16_SUBLANES)
                accs[i % n_acc] = accs[i % n_acc] + jnp.where(sbf_ref[rows, :] >= cand, one, zero)
            return tuple(accs)

        one = jnp.ones((BF16_SUBLANES, tq), BF16)
        zero = jnp.zeros((BF16_SUBLANES, tq), BF16)
        accs = lax.fori_loop(0, n_chunks, body, (zero,) * n_acc)
        total = (accs[0].astype(F32) + accs[1].astype(F32)) + (accs[2].astype(F32) + accs[3].astype(F32))
        return jnp.sum(total, axis=0, keepdims=True)

    def count(pred):
        n_acc = 4

        def body(c, accs):
            accs = list(accs)
            for i in range(ck // SUBLANES):
                rows = pl.ds(pl.multiple_of(c * ck + i * SUBLANES, SUBLANES), SUBLANES)
                accs[i % n_acc] = accs[i % n_acc] + pred(sc_ref[rows, :]).astype(F32)
            return tuple(accs)

        zero = jnp.zeros((SUBLANES, tq), F32)
        accs = lax.fori_loop(0, n_chunks, body, (zero,) * n_acc)
        return jnp.sum((accs[0] + accs[1]) + (accs[2] + accs[3]), axis=0, keepdims=True)

    coarse, _ = _bisect_key(count_rounded, (1, tq), topk, n_bits=16)
    u, cnt_u = _bisect_key(lambda key: count(lambda s: s >= _key_to_float(key)), (1, tq), topk,
                           base=coarse - KEY_HALF_ULP, hi_bit=16, n_bits=17)
    small = t < topk
    thr = jnp.where(small, F32_LOWEST, _key_to_float(u))
    excess = jnp.where(cnt_u == 0.0, 1.0, cnt_u - topk)
    has_ties = jnp.max(jnp.where(small, 0.0, excess)) > 0.0

    @pl.when(has_ties)
    def _():
        need = topk - count(lambda s: s > thr)

        def select_chunk(c, carry):
            bias, carry = _select_chunk_t(sc_ref[chunk(c), :], thr, need, carry, tril_ref[...])
            sc_ref[chunk(c), :] = bias
            return carry

        lax.fori_loop(0, n_chunks, select_chunk, jnp.zeros((1, tq), F32))

    @pl.when(jnp.logical_not(has_ties))
    def _():
        def select_chunk(c, carry):
            sc_ref[chunk(c), :] = _threshold_bias(sc_ref[chunk(c), :], thr)
            return carry

        lax.fori_loop(0, n_chunks, select_chunk, 0)

    q = q_ref[0]
    ones = jnp.ones((BF16_SUBLANES, ck), BF16)
    q2s = [jnp.concatenate([q[:, (g * Q_PER_KV + i) * HEAD_DIM:(g * Q_PER_KV + i + 1) * HEAD_DIM]
                            for i in range(Q_PER_KV)], axis=0) for g in range(N_KV_HEADS)]

    def key_norm_chunk(c, acc):
        return jnp.maximum(acc, jnp.max(kn_ref[0, chunk(c), :], axis=0, keepdims=True))

    kn_max = lax.fori_loop(0, n_chunks, key_norm_chunk, jnp.zeros((1, LANES), F32))
    k_norm = jnp.sqrt(kn_max) * (BOUND_SLACK * BF16_ROUND_UP)
    bound = jnp.concatenate([qn0_ref[0] * k_norm, qn1_ref[0] * k_norm], axis=0) + BOUND_FLOOR
    shift = (-bound).astype(BF16)
    q2_shifted = [jnp.concatenate([q2s[g], shift], axis=1) for g in range(N_KV_HEADS)]
    lane_k = lax.broadcasted_iota(jnp.int32, (ck, HEAD_DIM), 1)
    pick = [jnp.where(lane_k == g, 1.0, 0.0).astype(BF16) for g in range(N_KV_HEADS)]
    acc_ref[...] = jnp.zeros(acc_ref.shape, F32)

    def attend_shifted(c, carry):
        bias = jnp.concatenate([sc_ref[chunk(c), :]] * Q_PER_KV, axis=1)
        for g in range(N_KV_HEADS):
            gs = slice(g * HEAD_DIM, (g + 1) * HEAD_DIM)
            k1 = jnp.concatenate([kb_ref[0, chunk(c), gs], pick[g]], axis=1)
            p = jnp.exp2(_dot_nt(k1, q2_shifted[g]) + bias).astype(BF16)
            v1 = jnp.concatenate([vt_ref[0, gs, chunk(c)], ones], axis=0)
            acc_ref[g] = acc_ref[g] + _dot(v1, p)
        return carry

    lax.fori_loop(0, n_chunks, attend_shifted, 0)
    denom_min = acc_ref[0][HEAD_DIM:HEAD_DIM + 1]
    for g in range(1, N_KV_HEADS):
        denom_min = jnp.minimum(denom_min, acc_ref[g][HEAD_DIM:HEAD_DIM + 1])
    underflow = jnp.logical_not(jnp.min(denom_min) > DENOM_FLOOR)

    @pl.when(underflow)
    def _():
        _attend_running_max(q2s, kb_ref, vt_ref, sc_ref, m_ref, acc_ref, n_chunks, chunk, ones)

    for g in range(N_KV_HEADS):
        acc = acc_ref[g]
        o_t = acc[:HEAD_DIM] / acc[HEAD_DIM:HEAD_DIM + 1]
        for i in range(Q_PER_KV):
            h = g * Q_PER_KV + i
            o_ref[0, :, h * HEAD_DIM:(h + 1) * HEAD_DIM] = o_t[:, i * tq:(i + 1) * tq].T.astype(BF16)


def _attend_running_max(q2s, kb_ref, vt_ref, sc_ref, m_ref, acc_ref, n_chunks, chunk, ones):
    m_ref[...] = jnp.full(m_ref.shape, M_INIT, F32)
    acc_ref[...] = jnp.zeros(acc_ref.shape, F32)

    def attend_chunk(c, carry):
        bias = jnp.concatenate([sc_ref[chunk(c), :]] * Q_PER_KV, axis=1)
        for g in range(N_KV_HEADS):
            gs = slice(g * HEAD_DIM, (g + 1) * HEAD_DIM)
            s = _dot_nt(kb_ref[0, chunk(c), gs], q2s[g]) + bias
            m_old = m_ref[g][:1]
            m_new = jnp.maximum(m_old, jnp.max(s, axis=0, keepdims=True))
            p = jnp.exp2(s - m_new).astype(BF16)
            v1 = jnp.concatenate([vt_ref[0, gs, chunk(c)], ones], axis=0)
            acc_ref[g] = jnp.exp2(m_old - m_new) * acc_ref[g] + _dot(v1, p)
            m_ref[g] = jnp.broadcast_to(m_new, m_ref.shape[1:])
        return carry

    lax.fori_loop(0, n_chunks, attend_chunk, 0)


def _prompt_attn(q, qi, kw, qn0, qn1, kb, kn, vt, tril):
    b, s, _ = q.shape
    tq = Q_TILE
    assert s % KEY_CHUNK == 0 and s % tq == 0 and tq % LANES == 0
    topk = min(TOPK_MAX, s // 4)
    blk = lambda n: pl.BlockSpec((1, tq, n), lambda i, j: (i, j, 0))
    seq = lambda n: pl.BlockSpec((1, s, n), lambda i, j: (i, 0, 0))
    return pl.pallas_call(
        functools.partial(_prompt_attn_kernel, topk=topk),
        grid=(b, s // tq),
        in_specs=[blk(D_MODEL), blk(N_IDX_HEADS * IDX_DIM), blk(LANES), blk(LANES), blk(LANES), seq(KV_DIM),
                  seq(LANES),
                  pl.BlockSpec((1, KV_DIM, s), lambda i, j: (i, 0, 0)), seq(LANES), _resident(tril.shape)],
        out_specs=blk(D_MODEL),
        out_shape=jax.ShapeDtypeStruct((b, s, D_MODEL), BF16),
        scratch_shapes=[pltpu.VMEM((s, tq), F32),
                        pltpu.VMEM((s, tq), BF16),
                        pltpu.VMEM((N_KV_HEADS, SUBLANES, Q_PER_KV * tq), F32),
                        pltpu.VMEM((N_KV_HEADS, HEAD_DIM + BF16_SUBLANES, Q_PER_KV * tq), F32)],
        compiler_params=pltpu.CompilerParams(dimension_semantics=("arbitrary", "arbitrary"),
                                             vmem_limit_bytes=VMEM_LIMIT_BYTES),
        name="prompt_attn",
    )(q, qi, kw, qn0, qn1, kb, kn, vt, kw, tril)


def _sample_score_kernel(pt_ref, qi_ref, w_ref, knew_ref, *refs, n_pages, t_new):
    del pt_ref
    page_refs = refs[:n_pages]
    sp_ref, sn_ref = refs[n_pages:]
    qi = qi_ref[0]
    w = (w_ref[0] * (N_IDX_HEADS ** -0.5)) * (IDX_DIM ** -0.5)

    def head_sum(dots):
        r = jnp.maximum(dots, 0.0) * w
        return [jnp.sum(r[i * N_IDX_HEADS:(i + 1) * N_IDX_HEADS], axis=0, keepdims=True)
                for i in range(t_new)]

    group = 4
    for p0 in range(0, n_pages, group):
        keys_t = jnp.concatenate([page_refs[p][0] for p in range(p0, p0 + group)], axis=1).astype(BF16)
        for i, r in enumerate(head_sum(_dot(qi, keys_t))):
            sp_ref[0, i:i + 1, p0 * PAGE_SIZE:(p0 + group) * PAGE_SIZE] = r

    knew = jnp.concatenate([knew_ref[0][:, :IDX_DIM],
                            jnp.zeros((LANES - knew_ref.shape[1], IDX_DIM), F32)], axis=0).astype(BF16)
    kpos = lax.broadcasted_iota(jnp.int32, (1, LANES), 1)
    for i, r in enumerate(head_sum(_dot_nt(qi, knew))):
        sn_ref[0, i:i + 1, :] = jnp.where(kpos <= i, r, NEG_INF)


def _sample_scores(page_table, qi32, w32, knew, cache_kidx_t):
    db, n_pages = page_table.shape
    t_new = qi32.shape[1] // N_IDX_HEADS
    assert n_pages % 4 == 0
    page_spec = lambda p: pl.BlockSpec((1, IDX_DIM, PAGE_SIZE), lambda b, pt: (pt[b, p], 0, 0))
    grid_spec = pltpu.PrefetchScalarGridSpec(
        num_scalar_prefetch=1,
        grid=(db,),
        in_specs=[pl.BlockSpec((1,) + qi32.shape[1:], lambda b, pt: (b, 0, 0)),
                  pl.BlockSpec((1,) + w32.shape[1:], lambda b, pt: (b, 0, 0)),
                  pl.BlockSpec((1,) + knew.shape[1:], lambda b, pt: (b, 0, 0))]
                 + [page_spec(p) for p in range(n_pages)],
        out_specs=[pl.BlockSpec((1, t_new, n_pages * PAGE_SIZE), lambda b, pt: (b, 0, 0)),
                   pl.BlockSpec((1, t_new, LANES), lambda b, pt: (b, 0, 0))],
    )
    return pl.pallas_call(
        functools.partial(_sample_score_kernel, n_pages=n_pages, t_new=t_new),
        grid_spec=grid_spec,
        out_shape=[jax.ShapeDtypeStruct((db, t_new, n_pages * PAGE_SIZE), F32),
                   jax.ShapeDtypeStruct((db, t_new, LANES), F32)],
        compiler_params=pltpu.CompilerParams(dimension_semantics=("arbitrary",),
                                             vmem_limit_bytes=VMEM_LIMIT_BYTES),
        name="sample_scores",
    )(page_table, qi32, w32, knew, *([cache_kidx_t] * n_pages))


def _sample_select_kernel(sp_ref, sn_ref, tri_ref, slot_ref, bn_ref, npast_ref, *, topk):
    rows, past = sp_ref.shape
    ck = KEY_CHUNK
    n_chunks = past // ck

    def count(pred):
        def body(c, cnt):
            s = sp_ref[:, pl.ds(pl.multiple_of(c * ck, ck), ck)]
            return cnt + _lane_partial(pred(s).astype(F32))
        part = lax.fori_loop(0, n_chunks, body, pred(sn_ref[...]).astype(F32))
        return jnp.sum(part, axis=-1, keepdims=True)

    u, _ = _kth_largest_key(lambda cand: count(lambda s: s >= cand), (rows, 1), topk)
    thr = _key_to_float(u)
    need = topk - count(lambda s: s > thr)

    def select_chunk(c, carry):
        ties, taken = carry
        cs = pl.ds(pl.multiple_of(c * ck, ck), ck)
        bias, ties = _select_chunk(sp_ref[:, cs], thr, need, ties, tri_ref[...])
        sel = (bias == 0.0).astype(F32)
        inc = _dot(sel.astype(BF16), tri_ref[...])
        slot_ref[:, cs] = jnp.where(sel > 0.0, taken + inc - 1.0, -1.0).astype(jnp.int32)
        return ties, taken + inc[:, ck - 1:]

    zero = jnp.zeros((rows, 1), F32)
    ties, taken = lax.fori_loop(0, n_chunks, select_chunk, (zero, zero))
    bias, _ = _select_chunk(sn_ref[...], thr, need, ties, tri_ref[:LANES, :LANES])
    bn_ref[...] = bias
    npast_ref[...] = jnp.broadcast_to(taken, npast_ref.shape)


def _sample_select(sp, sn, tri, *, topk):
    n, past = sp.shape
    rows = min(SAMPLE_ROWS, n)
    assert n % rows == 0 and past % KEY_CHUNK == 0
    lane_spec = pl.BlockSpec((rows, LANES), lambda i: (i, 0))
    return pl.pallas_call(
        functools.partial(_sample_select_kernel, topk=topk),
        grid=(n // rows,),
        in_specs=[pl.BlockSpec((rows, past), lambda i: (i, 0)), lane_spec, _resident(tri.shape)],
        out_specs=[pl.BlockSpec((rows, past), lambda i: (i, 0)), lane_spec, lane_spec],
        out_shape=[jax.ShapeDtypeStruct((n, past), jnp.int32), jax.ShapeDtypeStruct((n, LANES), F32),
                   jax.ShapeDtypeStruct((n, LANES), F32)],
        compiler_params=pltpu.CompilerParams(dimension_semantics=("arbitrary",),
                                             vmem_limit_bytes=VMEM_LIMIT_BYTES),
        name="sample_select",
    )(sp, sn, tri)


def _sample_gather(slot, page_table, table_k, table_v, *, topk):
    nq, n_pages, _ = slot.shape
    db = page_table.shape[0]
    t_new = nq // db
    workers = SC_CORES * SC_SUBCORES
    assert nq % workers == 0 and topk % SC_GATHER_WINDOW == 0 and PAGE_SIZE % SC_LANES == 0
    assert t_new & (t_new - 1) == 0
    per_worker = nq // workers
    row_shape = (N_KV_HEADS, HEAD_DIM)
    out = jax.ShapeDtypeStruct((nq, topk) + row_shape, F32)
    mesh = plsc.VectorSubcoreMesh(core_axis_name="c", subcore_axis_name="s")

    @functools.partial(
        pl.kernel, mesh=mesh, out_type=(out, out),
        scratch_types=[pltpu.VMEM((n_pages, PAGE_SIZE), jnp.int32),
                       pltpu.VMEM((n_pages,), jnp.int32),
                       pltpu.VMEM((topk,), jnp.int32),
                       pltpu.VMEM((SC_GATHER_WINDOW,) + row_shape, F32)],
        compiler_params=pltpu.CompilerParams(needs_layout_passes=False),
        name="sample_gather")
    def gather(slot_hbm, pt_hbm, tk_hbm, tv_hbm, kg_hbm, vg_hbm, slot_v, pt_v, idx_v, rows_v):
        worker = lax.axis_index("c") * SC_SUBCORES + lax.axis_index("s")
        lane = lax.iota(jnp.int32, SC_LANES)

        @pl.loop(0, per_worker)
        def _(i):
            q = worker * per_worker + i
            pltpu.sync_copy(slot_hbm.at[q], slot_v)
            pltpu.sync_copy(pt_hbm.at[lax.shift_right_logical(q, int(math.log2(t_new)))], pt_v)

            @pl.loop(0, topk // SC_LANES)
            def _(j):
                idx_v[pl.ds(j * SC_LANES, SC_LANES)] = jnp.zeros((SC_LANES,), jnp.int32)

            @pl.loop(0, n_pages)
            def _(p):
                base = plsc.load_gather(pt_v, [jnp.full((SC_LANES,), p, jnp.int32)]) * PAGE_SIZE
                for l in range(PAGE_SIZE // SC_LANES):
                    s = slot_v[p, pl.ds(l * SC_LANES, SC_LANES)]
                    plsc.store_scatter(idx_v, [s], base + (l * SC_LANES + lane), mask=s >= 0)

            for w in range(topk // SC_GATHER_WINDOW):
                win = pl.ds(w * SC_GATHER_WINDOW, SC_GATHER_WINDOW)
                for table, dst in ((tk_hbm, kg_hbm), (tv_hbm, vg_hbm)):
                    pltpu.sync_copy(table.at[idx_v.at[win]], rows_v)
                    pltpu.sync_copy(rows_v, dst.at[q, win])

    return gather(slot, page_table, table_k, table_v)


def _sample_attn_kernel(q_ref, kg_ref, vg_ref, knew_ref, vnew_ref, bn_ref, npast_ref, o_ref, *, t_new, topk):
    rows = t_new * Q_PER_KV
    n_keys = t_new * topk
    row_query = lax.shift_right_logical(lax.broadcasted_iota(jnp.int32, (rows, 1), 0), int(math.log2(Q_PER_KV)))
    col = lax.broadcasted_iota(jnp.int32, (1, n_keys), 1)
    col_query = lax.shift_right_logical(col, int(math.log2(topk)))
    col_slot = (col & (topk - 1)).astype(F32)
    own = (col_query == row_query) & (col_slot < npast_ref[0][:, :1])
    bias_past = jnp.where(own, 0.0, NEG_INF)
    bias_new = bn_ref[0]
    for h in range(N_KV_HEADS):
        hs = slice(h * HEAD_DIM, (h + 1) * HEAD_DIM)
        q = q_ref[0, h]
        keys = kg_ref[pl.ds(h, n_keys, stride=N_KV_HEADS), :].astype(BF16)
        vals = vg_ref[pl.ds(h, n_keys, stride=N_KV_HEADS), :].astype(BF16)
        s_past = _dot_nt(q, keys) + bias_past
        s_new = _dot_nt(q, knew_ref[0][:, hs]) + bias_new
        m = jnp.maximum(jnp.max(s_past, axis=-1, keepdims=True), jnp.max(s_new, axis=-1, keepdims=True))
        p_past = jnp.exp(s_past - m)
        p_new = jnp.exp(s_new - m)
        denom = jnp.sum(p_past, axis=-1, keepdims=True) + jnp.sum(p_new, axis=-1, keepdims=True)
        o = _dot(p_past.astype(BF16), vals) + _dot(p_new.astype(BF16), vnew_ref[0][:, hs])
        o_ref[0, h] = (o / denom).astype(BF16)


def _sample_attn(q8, kg, vg, knew, vnew, bias_new8, npast8, *, topk):
    db, _, rows, _ = q8.shape
    t_new = rows // Q_PER_KV
    assert topk & (topk - 1) == 0
    gathered = pl.BlockSpec((t_new * topk * N_KV_HEADS, HEAD_DIM), lambda b: (b, 0))
    per_seq = lambda arr: pl.BlockSpec((1,) + arr.shape[1:], lambda b: (b,) + (0,) * (arr.ndim - 1))
    return pl.pallas_call(
        functools.partial(_sample_attn_kernel, t_new=t_new, topk=topk),
        grid=(db,),
        in_specs=[per_seq(q8), gathered, gathered, per_seq(knew), per_seq(vnew), per_seq(bias_new8),
                  per_seq(npast8)],
        out_specs=per_seq(q8),
        out_shape=jax.ShapeDtypeStruct(q8.shape, BF16),
        compiler_params=pltpu.CompilerParams(dimension_semantics=("arbitrary",),
                                             vmem_limit_bytes=VMEM_LIMIT_BYTES),
        name="sample_attn",
    )(q8, kg, vg, knew, vnew, bias_new8, npast8)


def _sample_mixer_b(page_table, q, qi, kw, kb, vb, cache_k, cache_v, cache_kidx, tri):
    db, n_pages = page_table.shape
    n_pool = cache_k.shape[0]
    t_new = q.shape[0] // db
    past = n_pages * PAGE_SIZE
    topk = min(TOPK_MAX, (past + t_new) // 4)
    rows = t_new * N_IDX_HEADS
    qi32 = qi.reshape(db, rows, IDX_DIM)
    w32 = kw[:, IDX_DIM:IDX_DIM + N_IDX_HEADS].reshape(db, rows, 1)
    knew_idx = jnp.pad(kw.reshape(db, t_new, LANES), ((0, 0), (0, SUBLANES - t_new), (0, 0)))
    sp, sn = _sample_scores(page_table, qi32, w32, knew_idx, jnp.swapaxes(cache_kidx, 1, 2))
    slot, bias_n, npast = _sample_select(sp.reshape(db * t_new, past), sn.reshape(db * t_new, LANES), tri,
                                         topk=topk)
    kg, vg = _sample_gather(slot.reshape(db * t_new, n_pages, PAGE_SIZE), page_table,
                            cache_k.reshape(n_pool * PAGE_SIZE, N_KV_HEADS, HEAD_DIM),
                            cache_v.reshape(n_pool * PAGE_SIZE, N_KV_HEADS, HEAD_DIM), topk=topk)
    q8 = q.reshape(db, t_new, N_KV_HEADS, Q_PER_KV, HEAD_DIM).transpose(0, 2, 1, 3, 4)
    q8 = q8.reshape(db, N_KV_HEADS, t_new * Q_PER_KV, HEAD_DIM)
    per_row = lambda arr: jnp.repeat(arr.reshape(db, t_new, LANES), Q_PER_KV, axis=1)
    pad_new = lambda arr: jnp.pad(arr.reshape(db, t_new, KV_DIM), ((0, 0), (0, NEW_ROWS - t_new), (0, 0)))
    flat = lambda arr: arr.reshape(db * t_new * topk * N_KV_HEADS, HEAD_DIM)
    o8 = _sample_attn(q8, flat(kg), flat(vg), pad_new(kb), pad_new(vb), per_row(bias_n), per_row(npast),
                      topk=topk)
    o = o8.reshape(db, N_KV_HEADS, t_new, Q_PER_KV, HEAD_DIM).transpose(0, 2, 1, 3, 4)
    return o.reshape(db * t_new, D_MODEL)


def _prepare_weights(layer, w_in, ln_v_g, ln_v_b, w_s, b_s, w_pa, w_pb, w_o, ln1_g, ln1_b, w_gate, w_up,
                     w_down, ln2_g, ln2_b, t_new):
    w = w_in[layer]
    o_q = 2 * D_MODEL
    o_kv = o_q + D_MODEL
    o_qi = o_kv + 2 * KV_DIM
    o_ki = o_qi + N_IDX_HEADS * IDX_DIM
    o_g = o_ki + IDX_DIM + N_IDX_HEADS
    row = lambda v: v[layer].reshape(1, -1)
    reps = CHUNK // t_new
    shared = dict(
        wuv=w[:, :o_q].astype(BF16), wq=w[:, o_q:o_kv].astype(BF16), wkv=w[:, o_kv:o_qi].astype(BF16),
        wqi=w[:, o_qi:o_ki].astype(BF16),
        wkw=jnp.pad(w[:, o_ki:o_g], ((0, 0), (0, LANES - (o_g - o_ki)))).astype(BF16),
        wg=w[:, o_g:].astype(BF16),
        ln_v_g=row(ln_v_g), ln_v_b=row(ln_v_b), wpa=w_pa[layer].astype(BF16),
        wpb=w_pb[layer].astype(BF16), wo=w_o[layer].astype(BF16), ln1_g=row(ln1_g), ln1_b=row(ln1_b),
        wgate=w_gate[layer].astype(BF16), wup=w_up[layer].astype(BF16), wdown=w_down[layer].astype(BF16),
        ln2_g=row(ln2_g), ln2_b=row(ln2_b))
    prompt = dict(shared, wmix=w_s[layer].astype(BF16),
                  bmix=jnp.repeat(b_s[layer].T, A_GROUP_DIM, axis=1))
    sample = dict(shared, wmix=jnp.tile(w_s[layer][:, :t_new, :t_new], (1, reps, reps)).astype(BF16),
                  bmix=jnp.repeat(jnp.tile(b_s[layer][:, :t_new].T, (reps, 1)), A_GROUP_DIM, axis=1))
    return prompt, sample


def kernel(x_prompt, x_sample, cache_k, cache_v, cache_kidx, page_table, w_in, ln_v_g, ln_v_b, w_s, b_s,
           w_pa, w_pb, w_o, ln1_g, ln1_b, w_gate, w_up, w_down, ln2_g, ln2_b):
    depth = w_in.shape[0]
    alpha = (2 * depth) ** 0.25
    b, s, _ = x_prompt.shape
    db, t_new, _ = x_sample.shape
    n_pages = page_table.shape[1]
    n_pool = cache_k.shape[1]
    past = n_pages * PAGE_SIZE
    assert t_new <= CHUNK and CHUNK % t_new == 0 and t_new <= NEW_ROWS
    tri = jnp.triu(jnp.ones((KEY_CHUNK, KEY_CHUNK), BF16))
    tril = jnp.tril(jnp.ones((KEY_CHUNK, KEY_CHUNK), BF16))

    y_p = x_prompt.reshape(b * s, D_MODEL)
    y_s = x_sample.reshape(db * t_new, D_MODEL)
    outs = [[] for _ in range(7)]
    for layer in range(depth):
        wp, ws = _prepare_weights(layer, w_in, ln_v_g, ln_v_b, w_s, b_s, w_pa, w_pb, w_o, ln1_g, ln1_b,
                                  w_gate, w_up, w_down, ln2_g, ln2_b, t_new)

        a, sgb, q, k_p, v_p, kb, vt, qi, kw, kn, qn0, qn1 = _input_proj(y_p, wp, chunk_len=CHUNK,
                                                                        emit_vnorm=False, seq_len=s)
        seq = lambda arr: arr.reshape(b, s, arr.shape[-1])
        bo = _prompt_attn(seq(q), seq(qi), seq(kw), seq(qn0), seq(qn1), seq(kb), seq(kn), vt, tril)
        y_p = _merge_ffn(y_p, a, sgb, bo.reshape(b * s, D_MODEL), wp, alpha=alpha, name="merge_ffn_prompt")
        outs[0].append(k_p.reshape(b, s, N_KV_HEADS, HEAD_DIM))
        outs[1].append(v_p.reshape(b, s, N_KV_HEADS, HEAD_DIM))
        outs[2].append(kw[:, :IDX_DIM].reshape(b, s, IDX_DIM))

        a, sgb, q, k_s, v_s, kb, vb, qi, kw, _, _, _, vn = _input_proj(y_s, ws, chunk_len=t_new, emit_vnorm=True)
        bo = _sample_mixer_b(page_table, q, qi, kw, kb, vb, cache_k[layer], cache_v[layer], cache_kidx[layer],
                             tri)
        y_s = _merge_ffn(y_s, a, sgb, bo, ws, alpha=alpha, name="merge_ffn_sample")
        outs[3].append(k_s.reshape(db, t_new, N_KV_HEADS, HEAD_DIM))
        outs[4].append(v_s.reshape(db, t_new, N_KV_HEADS, HEAD_DIM))
        outs[5].append(kw[:, :IDX_DIM].reshape(db, t_new, IDX_DIM))
        outs[6].append(vn.reshape(db, t_new, D_MODEL))

    return (y_p.reshape(b, s, D_MODEL), y_s.reshape(db, t_new, D_MODEL), *[jnp.stack(o) for o in outs])
```
